```python
import jax, jax.numpy as jnp
from jax import lax
import numpy as np

D_MODEL = 2048
BATCH = 4
SEQ = 2048
DEPTH = 4
DEC_BATCH = 128
DEC_SEQ = 8
PAST_LEN = 16384
PAGE_SIZE = 128

HEAD = 64
D_R = D_MODEL // 2
H_R = D_R // HEAD
R_W = 64
R_A = 64
D_SHIFT = 3 * D_R + R_W + R_A
CHUNK = 128
D_G = D_MODEL // 2
N_GROUPS = 8
GW = D_G // N_GROUPS
N_IN = D_SHIFT + D_R + 3 * D_G + 2 * D_MODEL
RMS_EPS = 1e-6
LN_EPS = 1e-5
GN_EPS = 64e-5

kernel_name = "rwkv7_chunk_sgu_gated_hybrid_step"


def rms_norm(x, g):
    xf = x.astype(jnp.float32)
    y = xf * lax.rsqrt(jnp.mean(xf * xf, axis=-1, keepdims=True) + RMS_EPS)
    return (y * g).astype(x.dtype)


def layer_norm(x, g, b):
    xf = x.astype(jnp.float32)
    mu = jnp.mean(xf, axis=-1, keepdims=True)
    var = jnp.mean(jnp.square(xf - mu), axis=-1, keepdims=True)
    return ((xf - mu) * lax.rsqrt(var + LN_EPS) * g + b).astype(x.dtype)


def wkv7_scan(S0, r, w, k, v, a, b):
    def step(S, inp):
        r_t, w_t, k_t, v_t, a_t, b_t = inp
        Sa = jnp.einsum('bhvk,bhk->bhv', S, a_t)
        S = S * w_t[:, :, None, :] + Sa[..., None] * b_t[:, :, None, :] + v_t[..., None] * k_t[:, :, None, :]
        y = jnp.einsum('bhvk,bhk->bhv', S, r_t)
        return S, y
    xs = tuple(jnp.swapaxes(t, 0, 1) for t in (r, w, k, v, a, b))
    S, ys = lax.scan(step, S0, xs)
    return S, jnp.swapaxes(ys, 0, 1)


def chunk_spatial_mix(vn, sgu_w, sgu_b):
    Bn, T, _ = vn.shape
    L = CHUNK if T % CHUNK == 0 else T
    C = T // L
    vr = vn.reshape(Bn, C, L, N_GROUPS, GW)
    mask = jnp.tril(jnp.ones((L, L), vn.dtype))
    Wm = sgu_w[:, :L, :L] * mask
    s = jnp.einsum('gts,bcsgd->bctgd', Wm, vr) + jnp.transpose(sgu_b[:, :L])[None, None, :, :, None]
    return s.reshape(Bn, T, D_G)


def mixer_layer(h, shift_prev, S0, norm_g, w_in, shift_mu, w0, w2, a0, a2, k_k, k_a, r_k,
                lnx_g, lnx_b, sgu_ln_g, sgu_ln_b, sgu_w, sgu_b, w_proj_a, w_proj_b, w_out):
    Bn, T, _ = h.shape
    xn = rms_norm(h, norm_g)
    z = xn @ w_in
    zs = z[..., :D_SHIFT]
    zs_prev = jnp.concatenate([shift_prev.astype(z.dtype), zs[:, :-1]], axis=1)
    zs_mix = zs + (zs_prev - zs) * shift_mu
    new_shift = zs[:, -1:]
    r, k, v, wd, ad = jnp.split(zs_mix, [D_R, 2 * D_R, 3 * D_R, 3 * D_R + R_W], axis=-1)
    g_r, u, vg, g_g, gate_a, gate_b = jnp.split(
        z[..., D_SHIFT:], [D_R, D_R + D_G, D_R + 2 * D_G, D_R + 3 * D_G, D_R + 3 * D_G + D_MODEL], axis=-1)

    f32 = jnp.float32
    r, k, v, wd, ad = (t.astype(f32) for t in (r, k, v, wd, ad))
    w_log = -jax.nn.softplus(-(w0 + jnp.tanh(wd) @ w2)) - 0.5
    decay = jnp.exp(-jnp.exp(w_log))
    a_gate = jax.nn.sigmoid(a0 + ad @ a2)
    kk = (k * k_k).reshape(Bn, T, H_R, HEAD)
    kk = kk / jnp.maximum(jnp.linalg.norm(kk, axis=-1, keepdims=True), 1e-12)
    k = k * (1.0 + (a_gate - 1.0) * k_a)
    hs = lambda t: t.reshape(Bn, T, H_R, HEAD)
    rh, kh, vh, ah = hs(r), hs(k), hs(v), hs(a_gate)
    S, y = wkv7_scan(S0.astype(f32), rh, hs(decay), kh, vh, -kk, kk * ah)
    mu = jnp.mean(y, axis=-1, keepdims=True)
    var = jnp.mean(jnp.square(y - mu), axis=-1, keepdims=True)
    y = (y - mu) * lax.rsqrt(var + GN_EPS) * lnx_g.reshape(H_R, HEAD) + lnx_b.reshape(H_R, HEAD)
    y = y + jnp.sum(rh * kh * r_k, axis=-1, keepdims=True) * vh
    y = y.reshape(Bn, T, D_R).astype(h.dtype)
    ya = (y * jax.nn.silu(g_r)) @ w_proj_a

    vn = layer_norm(vg, sgu_ln_g, sgu_ln_b)
    s = chunk_spatial_mix(vn, sgu_w, sgu_b)
    yb = (u * s * jax.nn.silu(g_g)) @ w_proj_b

    m = jax.nn.sigmoid(gate_a) * ya + jax.nn.sigmoid(gate_b) * yb
    return h + m @ w_out, new_shift, S, vn


def setup_inputs(seed: int = 0) -> dict:
    key = jax.random.key(seed)
    ks = jax.random.split(key, 32)
    nrm = lambda k, shape, s: jax.random.normal(k, shape, jnp.float32) * s
    return {
        "x_prompt": nrm(ks[0], (BATCH, SEQ, D_MODEL), 1.0),
        "x_sample": nrm(ks[1], (DEC_BATCH, DEC_SEQ, D_MODEL), 1.0),
        "state_wkv": nrm(ks[2], (DEPTH, DEC_BATCH, H_R, HEAD, HEAD), 0.3),
        "state_shift": nrm(ks[3], (DEPTH, DEC_BATCH, 1, D_SHIFT), 1.0),
        "norm_g": 1.0 + nrm(ks[4], (DEPTH, D_MODEL), 0.05),
        "w_in": nrm(ks[5], (DEPTH, D_MODEL, N_IN), D_MODEL ** -0.5),
        "shift_mu": jax.random.uniform(ks[6], (DEPTH, D_SHIFT), jnp.float32, 0.1, 0.9),
        "w0": nrm(ks[7], (DEPTH, D_R), 0.5),
        "w2": nrm(ks[8], (DEPTH, R_W, D_R), 0.5 * R_W ** -0.5),
        "a0": nrm(ks[9], (DEPTH, D_R), 0.1),
        "a2": nrm(ks[10], (DEPTH, R_A, D_R), R_A ** -0.5),
        "k_k": 0.85 + nrm(ks[11], (DEPTH, D_R), 0.05),
        "k_a": 1.0 + nrm(ks[12], (DEPTH, D_R), 0.05),
        "r_k": nrm(ks[13], (DEPTH, H_R, HEAD), 0.1),
        "lnx_g": 1.0 + nrm(ks[14], (DEPTH, D_R), 0.05),
        "lnx_b": nrm(ks[15], (DEPTH, D_R), 0.02),
        "sgu_ln_g": 1.0 + nrm(ks[16], (DEPTH, D_G), 0.05),
        "sgu_ln_b": nrm(ks[17], (DEPTH, D_G), 0.02),
        "sgu_w": nrm(ks[18], (DEPTH, N_GROUPS, CHUNK, CHUNK), CHUNK ** -0.5),
        "sgu_b": 1.0 + nrm(ks[19], (DEPTH, N_GROUPS, CHUNK), 0.1),
        "w_proj_a": nrm(ks[20], (DEPTH, D_R, D_MODEL), D_R ** -0.5),
        "w_proj_b": nrm(ks[21], (DEPTH, D_G, D_MODEL), D_G ** -0.5),
        "w_out": nrm(ks[22], (DEPTH, D_MODEL, D_MODEL), D_MODEL ** -0.5),
        "final_norm_g": 1.0 + nrm(ks[23], (D_MODEL,), 0.05),
    }


def reference(x_prompt, x_sample, state_wkv, state_shift, norm_g, w_in, shift_mu, w0, w2, a0, a2,
              k_k, k_a, r_k, lnx_g, lnx_b, sgu_ln_g, sgu_ln_b, sgu_w, sgu_b, w_proj_a, w_proj_b,
              w_out, final_norm_g):
    hp, hs = x_prompt, x_sample
    wkv_p, shift_p, wkv_s, shift_s, chunk_v_s = [], [], [], [], []
    shift0 = jnp.zeros((BATCH, 1, D_SHIFT), x_prompt.dtype)
    S0 = jnp.zeros((BATCH, H_R, HEAD, HEAD), jnp.float32)
    for l in range(DEPTH):
        lp = (norm_g[l], w_in[l], shift_mu[l], w0[l], w2[l], a0[l], a2[l], k_k[l], k_a[l], r_k[l],
              lnx_g[l], lnx_b[l], sgu_ln_g[l], sgu_ln_b[l], sgu_w[l], sgu_b[l],
              w_proj_a[l], w_proj_b[l], w_out[l])
        hp, sh_p, S_p, _ = mixer_layer(hp, shift0, S0, *lp)
        hs, sh_s, S_s, vn_s = mixer_layer(hs, state_shift[l], state_wkv[l], *lp)
        wkv_p.append(S_p)
        shift_p.append(sh_p)
        wkv_s.append(S_s.astype(state_wkv.dtype))
        shift_s.append(sh_s.astype(state_shift.dtype))
        chunk_v_s.append(vn_s)
    y_prompt = rms_norm(hp, final_norm_g)
    y_sample = rms_norm(hs, final_norm_g)
    return (y_prompt, y_sample, jnp.stack(wkv_p), jnp.stack(shift_p), jnp.stack(wkv_s),
            jnp.stack(shift_s), jnp.stack(chunk_v_s))
```

```python
import functools
import math

import jax
import jax.numpy as jnp
from jax import lax
from jax.experimental import pallas as pl
from jax.experimental.pallas import tpu as pltpu

F32 = jnp.float32
BF16 = jnp.bfloat16

D_MODEL = 2048
BATCH = 4
SEQ = 2048
DEPTH = 4
DEC_BATCH = 128
DEC_SEQ = 8
HEAD = 64
D_R = 1024
H_R = 16
R_W = 64
R_A = 64
D_G = 1024
N_GROUPS = 8
GW = 128
RMS_EPS = 1e-6
LN_EPS = 1e-5
GN_EPS = 64e-5
EXP_M05 = math.exp(-0.5)

N_PROMPT = BATCH * SEQ
N_SAMPLE = DEC_BATCH * DEC_SEQ
N_TOK = N_PROMPT + N_SAMPLE

CB_U, CB_VG, CB_GG, CB_R, CB_K, CB_V, CB_GR = 4, 5, 6, 7, 8, 9, 10
CB_WA = 88
NZ = 11520

HG = 4
GL = HG * HEAD
N_HG = H_R // HG

WKV_L = 64
SAMPLE_NB = 8

TM_IN, TN_IN = 1024, 768
TM_OUT = 256
SGU_L = 128
VMEM_LIMIT = 56 * 1024 * 1024

V_MU_R, V_MU_K, V_MU_V, V_W0, V_A0, V_KK, V_KA, V_RK, V_LNG, V_LNB, V_MU_WA = range(11)
N_VEC = 16


def _mm(a, b):
    return jnp.dot(a.astype(BF16), b.astype(BF16), preferred_element_type=F32)


def _mm_nt(a, b):
    return lax.dot_general(a.astype(BF16), b.astype(BF16), (((1,), (1,)), ((), ())),
                           preferred_element_type=F32)


def _mm_tn(a, b):
    return lax.dot_general(a.astype(BF16), b.astype(BF16), (((0,), (0,)), ((), ())),
                           preferred_element_type=F32)


def _iota(shape, dim):
    return lax.broadcasted_iota(jnp.int32, shape, dim)


def _group_mask():
    return (_iota((GL, GL), 0) >> 6) == (_iota((GL, GL), 1) >> 6)


def _block_diag(x, gmask):
    reps = GL // x.shape[0]
    return jnp.where(gmask, jnp.concatenate([x] * reps, axis=0), 0.0)


def _segsum(x, pmat):
    xb = x.astype(BF16)
    return jnp.concatenate(
        [jnp.dot(xb[:, g * GL:(g + 1) * GL], pmat, preferred_element_type=F32) for g in range(N_HG)], axis=1)


def _cumsum_rows(tri, x):
    hi = x.astype(BF16)
    lo = (x - hi.astype(F32)).astype(BF16)
    return (jnp.dot(tri, hi, preferred_element_type=F32) + jnp.dot(tri, lo, preferred_element_type=F32))


def _inproj_kernel(x_ref, g_ref, w_ref, z_ref, xn_ref):
    @pl.when(pl.program_id(1) == 0)
    def _():
        x = x_ref[...]
        ms = jnp.mean(x * x, axis=-1, keepdims=True)
        xn_ref[...] = (x * lax.rsqrt(ms + RMS_EPS) * g_ref[...]).astype(BF16)

    z_ref[...] = jnp.dot(xn_ref[...], w_ref[...], preferred_element_type=F32)


def _inproj(h, norm_g, w_perm, layer):
    n_tok = h.shape[0]
    return pl.pallas_call(
        _inproj_kernel,
        grid=(n_tok // TM_IN, NZ // TN_IN),
        in_specs=[
            pl.BlockSpec((TM_IN, D_MODEL), lambda i, j: (i, 0)),
            pl.BlockSpec((None, 1, D_MODEL), lambda i, j: (layer, 0, 0)),
            pl.BlockSpec((None, D_MODEL, TN_IN), lambda i, j: (layer, 0, j)),
        ],
        out_specs=pl.BlockSpec((TM_IN, TN_IN), lambda i, j: (i, j)),
        out_shape=jax.ShapeDtypeStruct((n_tok, NZ), F32),
        scratch_shapes=[pltpu.VMEM((TM_IN, D_MODEL), BF16)],
        compiler_params=pltpu.CompilerParams(
            dimension_semantics=("arbitrary", "arbitrary"), vmem_limit_bytes=VMEM_LIMIT),
        name="inproj",
    )(h, norm_g, w_perm)


def _wkv_prep(r, k, v, wa, r_prev, k_prev, v_prev, wa_prev, vecs, w2a2, tri, pmat, last_row):
    vec = lambda i: vecs[i:i + 1, :]
    xr = r + (r_prev - r) * vec(V_MU_R)
    xk = k + (k_prev - k) * vec(V_MU_K)
    xv = v + (v_prev - v) * vec(V_MU_V)
    xwa = wa + (wa_prev - wa) * vecs[V_MU_WA:V_MU_WA + 1, :2 * R_W]
    lane = _iota(xwa.shape, 1)
    lowrank = _mm(jnp.where(lane < R_W, jnp.tanh(xwa), xwa), w2a2)
    lw = -EXP_M05 * jax.nn.sigmoid(vec(V_W0) + lowrank[:, :D_R])
    ag = jax.nn.sigmoid(vec(V_A0) + lowrank[:, D_R:])
    kk = xk * vec(V_KK)
    kkn = kk / jnp.maximum(jnp.sqrt(_segsum(kk * kk, pmat)), 1e-12)
    k2 = xk * (1.0 + (ag - 1.0) * vec(V_KA))
    cl = _cumsum_rows(tri, lw)
    cl_last = last_row(cl)
    e_neg = jnp.exp(-cl)
    e_rem = jnp.exp(cl_last - cl)
    bb = kkn * ag
    out = dict(
        rt=xr * jnp.exp(cl),
        kt=k2 * e_neg,
        at=-kkn * jnp.exp(cl - lw),
        bt=bb * e_neg,
        kd=k2 * e_rem,
        bd=bb * e_rem,
        v=xv,
        wl=jnp.exp(cl_last),
        bonus=_segsum(xr * k2 * vec(V_RK), pmat) * xv,
    )
    return out


def _wkv_finish(y, bonus, g_r, vecs, pmat):
    mu = _segsum(y, pmat) * (1.0 / HEAD)
    d = y - mu
    var = _segsum(d * d, pmat) * (1.0 / HEAD)
    yn = d * lax.rsqrt(var + GN_EPS) * vecs[V_LNG:V_LNG + 1, :] + vecs[V_LNB:V_LNB + 1, :]
    return ((yn + bonus) * (g_r * jax.nn.sigmoid(g_r))).astype(BF16)


def _wkv_prompt_kernel(r_ref, k_ref, v_ref, gr_ref, wa_ref, vecs_ref, w2a2_ref, tri_ref, pmat_ref,
                       y_ref, s_out_ref,
                       s_ref, pr_ref, pk_ref, pv_ref, pwa_ref,
                       rt_s, kt_s, at_s, bt_s, kd_s, bd_s, v_s, wl_s, y_s):
    c = pl.program_id(1)
    L = WKV_L

    @pl.when(c == 0)
    def _():
        s_ref[...] = jnp.zeros_like(s_ref)
        pr_ref[...] = jnp.zeros_like(pr_ref)
        pk_ref[...] = jnp.zeros_like(pk_ref)
        pv_ref[...] = jnp.zeros_like(pv_ref)
        pwa_ref[...] = jnp.zeros_like(pwa_ref)

    vecs = vecs_ref[...]
    pmat = pmat_ref[...]

    def prev_rows(x, p_ref):
        row0 = _iota(x.shape, 0) == 0
        return jnp.where(row0, p_ref[0:1, :], pltpu.roll(x, 1, 0))

    r, k, v, wa = r_ref[...], k_ref[...], v_ref[...], wa_ref[...]
    ops = _wkv_prep(r, k, v, wa, prev_rows(r, pr_ref), prev_rows(k, pk_ref), prev_rows(v, pv_ref),
                    prev_rows(wa, pwa_ref), vecs, w2a2_ref[...], tri_ref[...], pmat,
                    lambda x: jnp.broadcast_to(x[L - 1:L, :], x.shape))
    pr_ref[0:1, :] = r[L - 1:L, :]
    pk_ref[0:1, :] = k[L - 1:L, :]
    pv_ref[0:1, :] = v[L - 1:L, :]
    pwa_ref[0:1, :] = wa[L - 1:L, :]
    rt_s[...] = ops["rt"]
    kt_s[...] = ops["kt"]
    at_s[...] = ops["at"]
    bt_s[...] = ops["bt"]
    kd_s[...] = ops["kd"]
    bd_s[...] = ops["bd"]
    v_s[...] = ops["v"]
    wl_s[...] = ops["wl"][0:8, :]
    bonus = ops["bonus"]

    gmask = _group_mask()
    row = _iota((L, GL), 0)
    col = _iota((L, GL), 1) & (HEAD - 1)
    strict = row > col
    incl = row >= col
    eye = jnp.where(row == col, 1.0, 0.0)

    for g in range(N_HG):
        sl = slice(g * GL, (g + 1) * GL)
        at, rt, v_g = at_s[:, sl], rt_s[:, sl], v_s[:, sl]
        lhs = jnp.concatenate([at, rt], axis=0).astype(BF16)
        sb = _mm_nt(lhs, _block_diag(bt_s[:, sl], gmask))
        sk = _mm_nt(lhs, _block_diag(kt_s[:, sl], gmask))
        a_ab = jnp.where(strict, sb[:L], 0.0)
        a_ak = jnp.where(strict, sk[:L], 0.0)
        a_rb = jnp.where(incl, sb[L:], 0.0)
        a_rk = jnp.where(incl, sk[L:], 0.0)
        x = _block_diag(a_ab, gmask).astype(BF16)
        p = eye + a_ab
        for _ in range(int(math.log2(L)) - 1):
            x = jnp.dot(x, x, preferred_element_type=F32).astype(BF16)
            p = p + _mm(p, x)
        s0 = s_ref[g]
        gs = _mm_nt(lhs, s0)
        v_bd = _block_diag(v_g, gmask)
        u = _mm(p, _block_diag(gs[:L] + _mm(a_ak, v_bd), gmask))
        y_s[:, sl] = gs[L:] + _mm(a_rb, _block_diag(u, gmask)) + _mm(a_rk, v_bd)
        uv = jnp.concatenate([u, v_g], axis=0)
        zd = jnp.concatenate([bd_s[:, sl], kd_s[:, sl]], axis=0)
        s_ref[g] = jnp.where(gmask, s0 * wl_s[0:1, sl] + _mm_tn(uv, zd), 0.0)

    y_ref[...] = _wkv_finish(y_s[...], bonus, gr_ref[...], vecs, pmat)

    @pl.when(c == pl.num_programs(1) - 1)
    def _():
        for g in range(N_HG):
            for j in range(HG):
                s_out_ref[g * HG + j] = s_ref[g, j * HEAD:(j + 1) * HEAD, :][:, j * HEAD:(j + 1) * HEAD]


def _wkv_prompt(z, vecs, w2a2, tri, pmat, layer):
    L = WKV_L
    n_chunks = SEQ // L
    rows = lambda b, c: b * n_chunks + c
    zspec = lambda cb: pl.BlockSpec((L, D_R), lambda b, c: (rows(b, c), cb))
    full = lambda a: pl.BlockSpec(a.shape, lambda b, c: (0,) * a.ndim)
    row_scr = lambda w: pltpu.VMEM((8, w), F32)
    big = pltpu.VMEM((L, D_R), F32)
    return pl.pallas_call(
        _wkv_prompt_kernel,
        grid=(BATCH, n_chunks),
        in_specs=[
            zspec(CB_R), zspec(CB_K), zspec(CB_V), zspec(CB_GR),
            pl.BlockSpec((L, 2 * R_W), lambda b, c: (rows(b, c), CB_WA)),
            pl.BlockSpec((None, N_VEC, D_R), lambda b, c: (layer, 0, 0)),
            pl.BlockSpec((None, 2 * R_W, 2 * D_R), lambda b, c: (layer, 0, 0)),
            full(tri), full(pmat),
        ],
        out_specs=[
            pl.BlockSpec((L, D_R), lambda b, c: (rows(b, c), 0)),
            pl.BlockSpec((None, H_R, HEAD, HEAD), lambda b, c: (b, 0, 0, 0)),
        ],
        out_shape=[
            jax.ShapeDtypeStruct((N_PROMPT, D_R), BF16),
            jax.ShapeDtypeStruct((BATCH, H_R, HEAD, HEAD), F32),
        ],
        scratch_shapes=[
            pltpu.VMEM((N_HG, GL, GL), F32),
            row_scr(D_R), row_scr(D_R), row_scr(D_R), row_scr(2 * R_W),
            big, big, big, big, big, big, big, row_scr(D_R), big,
        ],
        compiler_params=pltpu.CompilerParams(
            dimension_semantics=("arbitrary", "arbitrary"), vmem_limit_bytes=VMEM_LIMIT),
        name="wkv_prompt",
    )(z, z, z, z, z, vecs, w2a2, tri, pmat)


def _wkv_sample_kernel(r_ref, k_ref, v_ref, gr_ref, wa_ref, pr_ref, pk_ref, pv_ref, pwa_ref, s_in_ref,
                       vecs_ref, w2a2_ref, tri_ref, pmat_ref,
                       y_ref, s_out_ref,
                       seg_s, u_s, zd_s):
    nb, T = SAMPLE_NB, DEC_SEQ
    R = nb * T
    vecs = vecs_ref[...]
    pmat = pmat_ref[...]

    def grp(x):
        return x.reshape(nb, T, x.shape[-1])

    def bc(x, j):
        x3 = grp(x)
        return jnp.broadcast_to(x3[:, j:j + 1, :], x3.shape).reshape(x.shape)

    def prev_rows(x, p_ref):
        first = (_iota(x.shape, 0) & (T - 1)) == 0
        prev = jnp.broadcast_to(p_ref[...], (nb, T, x.shape[-1])).reshape(x.shape)
        return jnp.where(first, prev, pltpu.roll(x, 1, 0))

    r, k, v, wa = r_ref[...], k_ref[...], v_ref[...], wa_ref[...]
    ops = _wkv_prep(r, k, v, wa, prev_rows(r, pr_ref), prev_rows(k, pk_ref), prev_rows(v, pv_ref),
                    prev_rows(wa, pwa_ref), vecs, w2a2_ref[...], tri_ref[...], pmat,
                    lambda x: bc(x, T - 1))
    rt, kt, at, bt, vv = ops["rt"], ops["kt"], ops["at"], ops["bt"], ops["v"]
    zd_s[...] = jnp.concatenate([ops["bd"], ops["kd"]], axis=0)

    for j in range(T):
        btj, ktj = bc(bt, j), bc(kt, j)
        seg_s[j, 0] = _segsum(at * btj, pmat)
        seg_s[j, 1] = _segsum(at * ktj, pmat)
        seg_s[j, 2] = _segsum(rt * btj, pmat)
        seg_s[j, 3] = _segsum(rt * ktj, pmat)

    gmask = _group_mask()
    ga_rows, gr_rows = [], []
    for b in range(nb):
        ga_g, gr_g = [], []
        for g in range(N_HG):
            sl = slice(g * GL, (g + 1) * GL)
            s4 = s_in_ref[b, g * HG:(g + 1) * HG].reshape(GL, HEAD)
            s_bd = jnp.where(gmask, jnp.concatenate([s4] * HG, axis=1), 0.0)
            lhs = jnp.concatenate([at[b * T:(b + 1) * T, sl], rt[b * T:(b + 1) * T, sl]], axis=0)
            gs = _mm_nt(lhs, s_bd)
            ga_g.append(gs[:T])
            gr_g.append(gs[T:])
        ga_rows.append(jnp.concatenate(ga_g, axis=1))
        gr_rows.append(jnp.concatenate(gr_g, axis=1))
    g_a = jnp.concatenate(ga_rows, axis=0)
    g_r = jnp.concatenate(gr_rows, axis=0)

    t_idx = _iota((R, D_R), 0) & (T - 1)
    u = g_a
    for j in range(T - 1):
        u = u + jnp.where(t_idx > j, seg_s[j, 1] * bc(vv, j), 0.0)
    for j in range(T - 1):
        u = u + jnp.where(t_idx > j, seg_s[j, 0] * bc(u, j), 0.0)
    y = g_r
    for j in range(T):
        y = y + jnp.where(t_idx >= j, seg_s[j, 2] * bc(u, j) + seg_s[j, 3] * bc(vv, j), 0.0)
    y_ref[...] = _wkv_finish(y, ops["bonus"], gr_ref[...], vecs, pmat)

    u_s[...] = jnp.concatenate([u, vv], axis=0)
    wl = ops["wl"]
    row2 = _iota((2 * R, GL), 0) & (R - 1)
    for g in range(N_HG):
        sl = slice(g * GL, (g + 1) * GL)
        uv_g = u_s[:, sl]
        zd_g = zd_s[:, sl]
        for b in range(nb):
            mine = (row2 >= b * T) & (row2 < (b + 1) * T)
            upd = _mm_tn(uv_g, jnp.where(mine, zd_g, 0.0))
            for j in range(HG):
                h = g * HG + j
                s_out_ref[b, h] = (s_in_ref[b, h] * wl[b * T:b * T + 1, h * HEAD:(h + 1) * HEAD]
                                   + upd[j * HEAD:(j + 1) * HEAD, j * HEAD:(j + 1) * HEAD])


def _wkv_sample(z, prev_r, prev_k, prev_v, prev_wa, s_in, vecs, w2a2, tri, pmat, layer):
    nb, T = SAMPLE_NB, DEC_SEQ
    R = nb * T
    row0 = N_PROMPT // R
    zspec = lambda cb: pl.BlockSpec((R, D_R), lambda i: (row0 + i, cb))
    pspec = lambda w: pl.BlockSpec((nb, 1, w), lambda i: (i, 0, 0))
    full = lambda a: pl.BlockSpec(a.shape, lambda i: (0,) * a.ndim)
    sspec = pl.BlockSpec((nb, H_R, HEAD, HEAD), lambda i: (i, 0, 0, 0))
    return pl.pallas_call(
        _wkv_sample_kernel,
        grid=(DEC_BATCH // nb,),
        in_specs=[
            zspec(CB_R), zspec(CB_K), zspec(CB_V), zspec(CB_GR),
            pl.BlockSpec((R, 2 * R_W), lambda i: (row0 + i, CB_WA)),
            pspec(D_R), pspec(D_R), pspec(D_R), pspec(2 * R_W),
            sspec,
            pl.BlockSpec((None, N_VEC, D_R), lambda i: (layer, 0, 0)),
            pl.BlockSpec((None, 2 * R_W, 2 * D_R), lambda i: (layer, 0, 0)),
            full(tri), full(pmat),
        ],
        out_specs=[pl.BlockSpec((R, D_R), lambda i: (i, 0)), sspec],
        out_shape=[
            jax.ShapeDtypeStruct((N_SAMPLE, D_R), BF16),
            jax.ShapeDtypeStruct((DEC_BATCH, H_R, HEAD, HEAD), F32),
        ],
        scratch_shapes=[
            pltpu.VMEM((T, 4, R, D_R), F32),
            pltpu.VMEM((2 * R, D_R), F32),
            pltpu.VMEM((2 * R, D_R), F32),
        ],
        compiler_params=pltpu.CompilerParams(
            dimension_semantics=("arbitrary",), vmem_limit_bytes=VMEM_LIMIT),
        name="wkv_sample",
    )(z, z, z, z, z, prev_r, prev_k, prev_v, prev_wa, s_in, vecs, w2a2, tri, pmat)


def _layer_norm(x, g, b):
    mu = jnp.mean(x, axis=-1, keepdims=True)
    d = x - mu
    var = jnp.mean(d * d, axis=-1, keepdims=True)
    return d * lax.rsqrt(var + LN_EPS) * g + b


def _sgu_prompt_kernel(u_ref, vg_ref, gg_ref, lng_ref, lnb_ref, w_ref, bias_ref, y_ref):
    L = SGU_L
    vn = _layer_norm(vg_ref[...], lng_ref[...], lnb_ref[...])
    causal = _iota((L, L), 0) >= _iota((L, L), 1)
    parts = []
    for g in range(N_GROUPS):
        wm = jnp.where(causal, w_ref[g], 0.0)
        parts.append(_mm(wm, vn[:, g * GW:(g + 1) * GW]))
    s = jnp.concatenate(parts, axis=1) + bias_ref[...]
    gg = gg_ref[...]
    y_ref[...] = (u_ref[...] * s * (gg * jax.nn.sigmoid(gg))).astype(BF16)


def _sgu_prompt(z, ln_g, ln_b, sgu_w, bias_exp, layer):
    L = SGU_L
    zspec = lambda cb: pl.BlockSpec((L, D_G), lambda i: (i, cb))
    vspec = pl.BlockSpec((None, 1, D_G), lambda i: (layer, 0, 0))
    return pl.pallas_call(
        _sgu_prompt_kernel,
        grid=(N_PROMPT // L,),
        in_specs=[
            zspec(CB_U), zspec(CB_VG), zspec(CB_GG), vspec, vspec,
            pl.BlockSpec((None, N_GROUPS, L, L), lambda i: (layer, 0, 0, 0)),
            pl.BlockSpec((None, L, D_G), lambda i: (layer, 0, 0)),
        ],
        out_specs=pl.BlockSpec((L, D_G), lambda i: (i, 0)),
        out_shape=jax.ShapeDtypeStruct((N_PROMPT, D_G), BF16),
        compiler_params=pltpu.CompilerParams(
            dimension_semantics=("arbitrary",), vmem_limit_bytes=VMEM_LIMIT),
        name="sgu_prompt",
    )(z, z, z, ln_g, ln_b, sgu_w, bias_exp)


def _sgu_sample_kernel(u_ref, vg_ref, gg_ref, lng_ref, lnb_ref, wc_ref, bias_ref, y_ref, vn_ref):
    T = DEC_SEQ
    R = u_ref.shape[0]
    nb = R // T
    vn = _layer_norm(vg_ref[...], lng_ref[...], lnb_ref[...])
    vn_ref[...] = vn
    vn3 = vn.reshape(nb, T, D_G)
    s3 = jnp.broadcast_to(bias_ref[...][None], (nb, T, D_G))
    for j in range(T):
        s3 = s3 + jnp.broadcast_to(vn3[:, j:j + 1, :], vn3.shape) * wc_ref[j][None]
    gg = gg_ref[...]
    y_ref[...] = (u_ref[...] * s3.reshape(R, D_G) * (gg * jax.nn.sigmoid(gg))).astype(BF16)


def _sgu_sample(z, ln_g, ln_b, wc, bias_c, layer):
    R = 128
    row0 = N_PROMPT // R
    zspec = lambda cb: pl.BlockSpec((R, D_G), lambda i: (row0 + i, cb))
    vspec = pl.BlockSpec((None, 1, D_G), lambda i: (layer, 0, 0))
    return pl.pallas_call(
        _sgu_sample_kernel,
        grid=(N_SAMPLE // R,),
        in_specs=[
            zspec(CB_U), zspec(CB_VG), zspec(CB_GG), vspec, vspec,
            pl.BlockSpec((None, DEC_SEQ, DEC_SEQ, D_G), lambda i: (layer, 0, 0, 0)),
            pl.BlockSpec((None, DEC_SEQ, D_G), lambda i: (layer, 0, 0)),
        ],
        out_specs=[pl.BlockSpec((R, D_G), lambda i: (i, 0)), pl.BlockSpec((R, D_G), lambda i: (i, 0))],
        out_shape=[jax.ShapeDtypeStruct((N_SAMPLE, D_G), BF16), jax.ShapeDtypeStruct((N_SAMPLE, D_G), F32)],
        compiler_params=pltpu.CompilerParams(
            dimension_semantics=("arbitrary",), vmem_limit_bytes=VMEM_LIMIT),
        name="sgu_sample",
    )(z, z, z, ln_g, ln_b, wc, bias_c)


def _outproj_kernel(ya_ref, yb_ref, ga_ref, gb_ref, h_ref, wa_ref, wb_ref, wo_ref, o_ref):
    ya = jnp.dot(ya_ref[...], wa_ref[...], preferred_element_type=F32)
    yb = jnp.dot(yb_ref[...], wb_ref[...], preferred_element_type=F32)
    m = jax.nn.sigmoid(ga_ref[...]) * ya + jax.nn.sigmoid(gb_ref[...]) * yb
    o_ref[...] = h_ref[...] + jnp.dot(m.astype(BF16), wo_ref[...], preferred_element_type=F32)


def _outproj(ya_in, yb_in, z, h, wpa, wpb, wout, layer):
    n_tok = h.shape[0]
    tm = TM_OUT
    wspec = lambda a: pl.BlockSpec((None,) + a.shape[1:], lambda i: (layer, 0, 0),
                                   pipeline_mode=pl.Buffered(1))
    return pl.pallas_call(
        _outproj_kernel,
        grid=(n_tok // tm,),
        in_specs=[
            pl.BlockSpec((tm, D_R), lambda i: (i, 0)),
            pl.BlockSpec((tm, D_G), lambda i: (i, 0)),
            pl.BlockSpec((tm, D_MODEL), lambda i: (i, 0)),
            pl.BlockSpec((tm, D_MODEL), lambda i: (i, 1)),
            pl.BlockSpec((tm, D_MODEL), lambda i: (i, 0)),
            wspec(wpa), wspec(wpb), wspec(wout),
        ],
        out_specs=pl.BlockSpec((tm, D_MODEL), lambda i: (i, 0)),
        out_shape=jax.ShapeDtypeStruct((n_tok, D_MODEL), F32),
        compiler_params=pltpu.CompilerParams(
            dimension_semantics=("arbitrary",), vmem_limit_bytes=VMEM_LIMIT),
        name="outproj",
    )(ya_in, yb_in, z, z, h, wpa, wpb, wout)


def _final_norm_kernel(x_ref, g_ref, o_ref):
    x = x_ref[...]
    ms = jnp.mean(x * x, axis=-1, keepdims=True)
    o_ref[...] = x * lax.rsqrt(ms + RMS_EPS) * g_ref[...]


def _final_norm(h, g):
    tm = 512
    return pl.pallas_call(
        _final_norm_kernel,
        grid=(h.shape[0] // tm,),
        in_specs=[pl.BlockSpec((tm, D_MODEL), lambda i: (i, 0)), pl.BlockSpec((1, D_MODEL), lambda i: (0, 0))],
        out_specs=pl.BlockSpec((tm, D_MODEL), lambda i: (i, 0)),
        out_shape=jax.ShapeDtypeStruct(h.shape, F32),
        compiler_params=pltpu.CompilerParams(dimension_semantics=("arbitrary",)),
        name="final_norm",
    )(h, g)


def _block_tri(n, blk):
    i = jnp.arange(n)
    return ((i[:, None] >= i[None, :]) & (i[:, None] // blk == i[None, :] // blk)).astype(BF16)


def kernel(x_prompt, x_sample, state_wkv, state_shift, norm_g, w_in, shift_mu, w0, w2, a0, a2, k_k, k_a, r_k,
           lnx_g, lnx_b, sgu_ln_g, sgu_ln_b, sgu_w, sgu_b, w_proj_a, w_proj_b, w_out, final_norm_g):
    c_r, c_wd, c_gr, c_u, c_ga = 0, 3 * D_R, 3 * D_R + R_W + R_A, 4 * D_R + R_W + R_A, 4 * D_R + R_W + R_A + 3 * D_G
    w_perm = jnp.concatenate([
        w_in[:, :, c_ga:], w_in[:, :, c_u:c_ga], w_in[:, :, c_r:c_wd], w_in[:, :, c_gr:c_u],
        w_in[:, :, c_wd:c_gr], jnp.zeros((DEPTH, D_MODEL, NZ - w_in.shape[2]), w_in.dtype)], axis=2).astype(BF16)
    wpa, wpb, wout = w_proj_a.astype(BF16), w_proj_b.astype(BF16), w_out.astype(BF16)

    zrow = jnp.zeros((DEPTH, D_R), F32)
    mu_wa = jnp.pad(shift_mu[:, c_wd:c_gr], ((0, 0), (0, D_R - 2 * R_W)))
    vec_rows = [shift_mu[:, 0:D_R], shift_mu[:, D_R:2 * D_R], shift_mu[:, 2 * D_R:3 * D_R], w0, a0, k_k, k_a,
                r_k.reshape(DEPTH, D_R), lnx_g, lnx_b, mu_wa] + [zrow] * (N_VEC - 11)
    vecs = jnp.stack(vec_rows, axis=1)
    zblk = jnp.zeros((DEPTH, R_W, D_R), F32)
    w2a2 = jnp.concatenate([jnp.concatenate([w2, zblk], axis=2),
                            jnp.concatenate([zblk, a2], axis=2)], axis=1).astype(BF16)
    norm_g3 = norm_g.reshape(DEPTH, 1, D_MODEL)
    ln_g3, ln_b3 = sgu_ln_g.reshape(DEPTH, 1, D_G), sgu_ln_b.reshape(DEPTH, 1, D_G)
    bias_exp = jnp.repeat(jnp.swapaxes(sgu_b, 1, 2), GW, axis=2)
    tmask = jnp.tril(jnp.ones((DEC_SEQ, DEC_SEQ), F32))
    wc = jnp.repeat(jnp.transpose(sgu_w[:, :, :DEC_SEQ, :DEC_SEQ] * tmask, (0, 3, 2, 1)), GW, axis=3)
    bias_c = bias_exp[:, :DEC_SEQ, :]

    pmat = ((jnp.arange(GL)[:, None] // HEAD) == (jnp.arange(GL)[None, :] // HEAD)).astype(BF16)
    tri_p = _block_tri(WKV_L, WKV_L)
    tri_s = _block_tri(SAMPLE_NB * DEC_SEQ, DEC_SEQ)

    prev = state_shift
    prev_r, prev_k, prev_v = prev[..., 0:D_R], prev[..., D_R:2 * D_R], prev[..., 2 * D_R:3 * D_R]
    prev_wa = prev[..., c_wd:c_gr]

    h = jnp.concatenate([x_prompt.reshape(N_PROMPT, D_MODEL), x_sample.reshape(N_SAMPLE, D_MODEL)], axis=0)
    wkv_p, shift_p, wkv_s, shift_s, chunk_v_s = [], [], [], [], []
    zs_cols = lambda zz: jnp.concatenate([zz[..., CB_R * D_R:(CB_V + 1) * D_R],
                                          zz[..., CB_WA * 128:CB_WA * 128 + 2 * R_W]], axis=-1)
    for l in range(DEPTH):
        z = _inproj(h, norm_g3, w_perm, l)
        ya_p, s_p = _wkv_prompt(z, vecs, w2a2, tri_p, pmat, l)
        ya_s, s_s = _wkv_sample(z, prev_r[l], prev_k[l], prev_v[l], prev_wa[l], state_wkv[l],
                                vecs, w2a2, tri_s, pmat, l)
        yb_p = _sgu_prompt(z, ln_g3, ln_b3, sgu_w, bias_exp, l)
        yb_s, vn_s = _sgu_sample(z, ln_g3, ln_b3, wc, bias_c, l)
        h = _outproj(jnp.concatenate([ya_p, ya_s], axis=0), jnp.concatenate([yb_p, yb_s], axis=0),
                     z, h, wpa, wpb, wout, l)
        wkv_p.append(s_p)
        wkv_s.append(s_s)
        shift_p.append(zs_cols(z[:N_PROMPT].reshape(BATCH, SEQ, NZ)[:, SEQ - 1:, :]))
        shift_s.append(zs_cols(z[N_PROMPT:].reshape(DEC_BATCH, DEC_SEQ, NZ)[:, DEC_SEQ - 1:, :]))
        chunk_v_s.append(vn_s.reshape(DEC_BATCH, DEC_SEQ, D_G))
    y = _final_norm(h, final_norm_g.reshape(1, D_MODEL))
    return (y[:N_PROMPT].reshape(BATCH, SEQ, D_MODEL), y[N_PROMPT:].reshape(DEC_BATCH, DEC_SEQ, D_MODEL),
            jnp.stack(wkv_p), jnp.stack(shift_p), jnp.stack(wkv_s), jnp.stack(shift_s), jnp.stack(chunk_v_s))
```

```python
import math

import jax
import jax.numpy as jnp
from jax import lax
from jax.experimental import pallas as pl
from jax.experimental.pallas import tpu as pltpu

F32 = jnp.float32
BF16 = jnp.bfloat16

D_MODEL = 2048
BATCH = 4
SEQ = 2048
DEPTH = 4
DEC_BATCH = 128
DEC_SEQ = 8
HEAD = 64
D_R = 1024
H_R = 16
R_W = 64
R_A = 64
D_G = 1024
N_GROUPS = 8
GW = 128
D_SHIFT = 3 * D_R + R_W + R_A
N_IN = D_SHIFT + D_R + 3 * D_G + 2 * D_MODEL
RMS_EPS = 1e-6
LN_EPS = 1e-5
GN_EPS = 64e-5
EXP_M05 = math.exp(-0.5)

N_PROMPT = BATCH * SEQ
N_SAMPLE = DEC_BATCH * DEC_SEQ

LANE = 128
CB_U, CB_VG, CB_GG, CB_R, CB_K, CB_V, CB_GR = 4, 5, 6, 7, 8, 9, 10
CB_WA = 88
TN_IN = 512
NZ = 23 * TN_IN
Z_SRC_UNITS = tuple([57 + 4 * j for j in range(8)] + [33 + 4 * j for j in range(6)] + [4 * j for j in range(6)]
                    + [25, 29] + [24])

HG = 4
GL = HG * HEAD
N_HG = H_R // HG

WKV_L = 64
SAMPLE_NB = 8

TM_IN_PROMPT, TM_IN_SAMPLE = 2048, 1024
TM_OUT = 256
SGU_L = 128
VMEM_LIMIT = 56 * 1024 * 1024

V_MU_R, V_MU_K, V_MU_V, V_W0, V_A0, V_KK, V_KA, V_RK, V_LNG, V_LNB, V_MU_WA = range(11)
N_VEC = 16


def _mm(a, b):
    return jnp.dot(a.astype(BF16), b.astype(BF16), preferred_element_type=F32)


def _mm_nt(a, b):
    return lax.dot_general(a.astype(BF16), b.astype(BF16), (((1,), (1,)), ((), ())),
                           preferred_element_type=F32)


def _mm_tn(a, b):
    return lax.dot_general(a.astype(BF16), b.astype(BF16), (((0,), (0,)), ((), ())),
                           preferred_element_type=F32)


def _iota(shape, dim):
    return lax.broadcasted_iota(jnp.int32, shape, dim)


def _block_diag(x, gmask_bf):
    reps = GL // x.shape[0]
    return jnp.concatenate([x.astype(BF16)] * reps, axis=0) * gmask_bf


def _segsum(x, pmat):
    rows = x.shape[0]
    xb = x.astype(BF16)
    stacked = jnp.concatenate([xb[:, g * GL:(g + 1) * GL] for g in range(N_HG)], axis=0)
    s = jnp.dot(stacked, pmat, preferred_element_type=F32)
    return jnp.concatenate([s[g * rows:(g + 1) * rows] for g in range(N_HG)], axis=1)


def _cumsum_rows(tri, x):
    hi = x.astype(BF16)
    lo = (x - hi.astype(F32)).astype(BF16)
    return (jnp.dot(tri, hi, preferred_element_type=F32) + jnp.dot(tri, lo, preferred_element_type=F32))


def _rms_scale(x, g):
    ms = jnp.mean(x * x, axis=-1, keepdims=True)
    return x * lax.rsqrt(ms + RMS_EPS) * g


def _params(*sem):
    return pltpu.CompilerParams(dimension_semantics=sem, vmem_limit_bytes=VMEM_LIMIT)


def _norm_kernel(x_ref, g_ref, o_ref):
    o_ref[...] = _rms_scale(x_ref[...], g_ref[...]).astype(o_ref.dtype)


def _norm_bf16(x, norm_g3, layer):
    tm = 512
    return pl.pallas_call(
        _norm_kernel,
        grid=(x.shape[0] // tm,),
        in_specs=[pl.BlockSpec((tm, D_MODEL), lambda i: (i, 0)),
                  pl.BlockSpec((None, 1, D_MODEL), lambda i: (layer, 0, 0))],
        out_specs=pl.BlockSpec((tm, D_MODEL), lambda i: (i, 0)),
        out_shape=jax.ShapeDtypeStruct(x.shape, BF16),
        compiler_params=_params("arbitrary"),
        name="norm_in",
    )(x, norm_g3)


def _inproj_kernel(cols_ref, x_ref, w_ref, z_ref):
    del cols_ref
    z_ref[...] = jnp.dot(x_ref[...], w_ref[...].astype(BF16), preferred_element_type=F32)


def _inproj(xn, w_in, src_cols, layer, tm):
    n_tok = xn.shape[0]
    grid_spec = pltpu.PrefetchScalarGridSpec(
        num_scalar_prefetch=1,
        grid=(n_tok // tm, NZ // TN_IN),
        in_specs=[
            pl.BlockSpec((tm, D_MODEL), lambda i, j, cols: (i, 0)),
            pl.BlockSpec((None, pl.Element(D_MODEL), pl.Element(TN_IN)),
                         lambda i, j, units: (layer, 0, units[j] * LANE)),
        ],
        out_specs=pl.BlockSpec((tm, TN_IN), lambda i, j, cols: (i, j)),
    )
    return pl.pallas_call(
        _inproj_kernel,
        grid_spec=grid_spec,
        out_shape=jax.ShapeDtypeStruct((n_tok, NZ), F32),
        compiler_params=_params("arbitrary", "arbitrary"),
        name="inproj",
    )(src_cols, xn, w_in)


def _wkv_prep(r, k, v, wa, r_prev, k_prev, v_prev, wa_prev, vecs, w2a2, tri, pmat, last_row):
    vec = lambda i: vecs[i:i + 1, :]
    xr = r + (r_prev - r) * vec(V_MU_R)
    xk = k + (k_prev - k) * vec(V_MU_K)
    xv = v + (v_prev - v) * vec(V_MU_V)
    xwa = wa + (wa_prev - wa) * vecs[V_MU_WA:V_MU_WA + 1, :2 * R_W]
    lane = _iota(xwa.shape, 1)
    lowrank = _mm(jnp.where(lane < R_W, jnp.tanh(xwa), xwa), w2a2)
    lw = -EXP_M05 * jax.nn.sigmoid(vec(V_W0) + lowrank[:, :D_R])
    ag = jax.nn.sigmoid(vec(V_A0) + lowrank[:, D_R:])
    kk = xk * vec(V_KK)
    k2 = xk * (1.0 + (ag - 1.0) * vec(V_KA))
    rows = r.shape[0]
    seg = _segsum(jnp.concatenate([kk * kk, xr * k2 * vec(V_RK)], axis=0), pmat)
    kkn = kk / jnp.maximum(jnp.sqrt(seg[:rows]), 1e-12)
    cl = _cumsum_rows(tri, lw)
    cl_last = last_row(cl)
    e_neg = jnp.exp(-cl)
    e_rem = jnp.exp(cl_last - cl)
    bb = kkn * ag
    return dict(
        rt=xr * jnp.exp(cl),
        kt=k2 * e_neg,
        at=-kkn * jnp.exp(cl - lw),
        bt=bb * e_neg,
        kd=k2 * e_rem,
        bd=bb * e_rem,
        v=xv,
        wl=jnp.exp(cl_last),
        bonus=seg[rows:] * xv,
    )


def _wkv_finish(y, bonus, g_r, vecs, pmat):
    mu = _segsum(y, pmat) * (1.0 / HEAD)
    d = y - mu
    var = _segsum(d * d, pmat) * (1.0 / HEAD)
    yn = d * lax.rsqrt(var + GN_EPS) * vecs[V_LNG:V_LNG + 1, :] + vecs[V_LNB:V_LNB + 1, :]
    return ((yn + bonus) * (g_r * jax.nn.sigmoid(g_r))).astype(BF16)


def _grouped(x, nb):
    return x.reshape(nb, x.shape[0] // nb, x.shape[-1])


def _bcast_row(x, nb, j):
    x3 = _grouped(x, nb)
    return jnp.broadcast_to(x3[:, j:j + 1, :], x3.shape).reshape(x.shape)


def _prev_rows(x, prev, nb):
    t = x.shape[0] // nb
    first = (_iota(x.shape, 0) & (t - 1)) == 0
    return jnp.where(first, jnp.broadcast_to(prev, (nb, t, x.shape[-1])).reshape(x.shape), pltpu.roll(x, 1, 0))


def _wkv_prompt_kernel(r_ref, k_ref, v_ref, gr_ref, wa_ref, vecs_ref, w2a2_ref, tri_ref, pmat_ref, gmask_ref,
                       *rest):
    y_ref, s_out_ref = rest[-17], rest[-16]
    (s_ref, pr_ref, pk_ref, pv_ref, pwa_ref,
     rt_s, kt_s, at_s, bt_s, kd_s, bd_s, v_s, wl_s, y_s, bonus_s) = rest[-15:]
    c = pl.program_id(0)
    L, nb = WKV_L, BATCH
    R = nb * L

    @pl.when(c == 0)
    def _():
        s_ref[...] = jnp.zeros_like(s_ref)
        pr_ref[...] = jnp.zeros_like(pr_ref)
        pk_ref[...] = jnp.zeros_like(pk_ref)
        pv_ref[...] = jnp.zeros_like(pv_ref)
        pwa_ref[...] = jnp.zeros_like(pwa_ref)

    vecs = vecs_ref[...]
    pmat = pmat_ref[...]
    gmask = gmask_ref[...]

    flat = lambda ref: ref[...].reshape(R, ref.shape[-1])
    r, k, v, wa = flat(r_ref), flat(k_ref), flat(v_ref), flat(wa_ref)
    ops = _wkv_prep(r, k, v, wa, _prev_rows(r, pr_ref[...], nb), _prev_rows(k, pk_ref[...], nb),
                    _prev_rows(v, pv_ref[...], nb), _prev_rows(wa, pwa_ref[...], nb),
                    vecs, w2a2_ref[...], tri_ref[...], pmat, lambda x: _bcast_row(x, nb, L - 1))
    pr_ref[...] = r_ref[:, L - 1:L, :]
    pk_ref[...] = k_ref[:, L - 1:L, :]
    pv_ref[...] = v_ref[:, L - 1:L, :]
    pwa_ref[...] = wa_ref[:, L - 1:L, :]
    rt_s[...] = ops["rt"]
    kt_s[...] = ops["kt"]
    at_s[...] = ops["at"]
    bt_s[...] = ops["bt"]
    kd_s[...] = ops["kd"]
    bd_s[...] = ops["bd"]
    v_s[...] = ops["v"]
    wl_s[...] = ops["wl"]
    bonus_s[...] = ops["bonus"]

    row = _iota((L, GL), 0)
    col = _iota((L, GL), 1) & (HEAD - 1)
    strict = row > col
    incl = row >= col
    eye = jnp.where(row == col, 1.0, 0.0)

    chains = [(b, g) for b in range(nb) for g in range(N_HG)]
    rs = lambda b: slice(b * L, (b + 1) * L)
    sl = lambda g: slice(g * GL, (g + 1) * GL)
    lhs = [jnp.concatenate([at_s[rs(b), sl(g)], rt_s[rs(b), sl(g)]], axis=0).astype(BF16) for b, g in chains]
    sc = [_mm_nt(lhs[n], jnp.concatenate([_block_diag(bt_s[rs(b), sl(g)], gmask),
                                          _block_diag(kt_s[rs(b), sl(g)], gmask)], axis=0))
          for n, (b, g) in enumerate(chains)]
    a_ab = [jnp.where(strict, s[:L, :GL], 0.0) for s in sc]
    a_ak = [jnp.where(strict, s[:L, GL:], 0.0) for s in sc]
    a_r = [jnp.concatenate([jnp.where(incl, s[L:, :GL], 0.0), jnp.where(incl, s[L:, GL:], 0.0)], axis=1)
           for s in sc]
    v_bd = [_block_diag(v_s[rs(b), sl(g)], gmask) for b, g in chains]
    gs = [_mm_nt(lhs[n], s_ref[b, g]) for n, (b, g) in enumerate(chains)]
    rhs = [gs[n][:L] + _mm(a_ak[n], v_bd[n]) for n in range(len(chains))]
    x = [_block_diag(a, gmask) for a in a_ab]
    p = [eye + a for a in a_ab]
    x = [jnp.dot(xn, xn, preferred_element_type=F32).astype(BF16) for xn in x]
    for _ in range(int(math.log2(L)) - 2):
        xp = [jnp.dot(jnp.concatenate([xn, pn.astype(BF16)], axis=0), xn, preferred_element_type=F32)
              for xn, pn in zip(x, p)]
        x = [v[:GL].astype(BF16) for v in xp]
        p = [pn + v[GL:] for pn, v in zip(p, xp)]
    p = [pn + _mm(pn, xn) for pn, xn in zip(p, x)]
    u = [_mm(pn, _block_diag(rn, gmask)) for pn, rn in zip(p, rhs)]
    for n, (b, g) in enumerate(chains):
        y_s[rs(b), sl(g)] = gs[n][L:] + _mm(a_r[n], jnp.concatenate([_block_diag(u[n], gmask), v_bd[n]], axis=0))
    gmask_f = gmask.astype(F32)
    for n, (b, g) in enumerate(chains):
        uv = jnp.concatenate([u[n], v_s[rs(b), sl(g)]], axis=0)
        zd = jnp.concatenate([bd_s[rs(b), sl(g)], kd_s[rs(b), sl(g)]], axis=0)
        s_ref[b, g] = (s_ref[b, g] * wl_s[b * L:b * L + 1, sl(g)] + _mm_tn(uv, zd)) * gmask_f

    y_ref[...] = _wkv_finish(y_s[...], bonus_s[...], flat(gr_ref), vecs, pmat).reshape(nb, L, D_R)

    @pl.when(c == pl.num_programs(0) - 1)
    def _():
        for b in range(nb):
            for g in range(N_HG):
                for j in range(HG):
                    s_out_ref[b, g * HG + j] = s_ref[b, g, j * HEAD:(j + 1) * HEAD, :][:, j * HEAD:(j + 1) * HEAD]


def _stack_io(buf, layer):
    if buf is None:
        return [], [], None
    return [pl.BlockSpec(memory_space=pl.ANY)], [buf], True


def _wkv_prompt(z3, vecs, w2a2, tri, pmat, gmask, s_buf, layer):
    L = WKV_L
    zspec = lambda cb: pl.BlockSpec((BATCH, L, D_R), lambda c: (0, c, cb))
    full = lambda a: pl.BlockSpec(a.shape, lambda c: (0,) * a.ndim)
    row_scr = lambda w: pltpu.VMEM((BATCH, 1, w), F32)
    big = pltpu.VMEM((BATCH * L, D_R), F32)
    extra_specs, extra_ops, alias = _stack_io(s_buf, layer)
    n_in = 10
    return pl.pallas_call(
        _wkv_prompt_kernel,
        grid=(SEQ // L,),
        in_specs=[
            zspec(CB_R), zspec(CB_K), zspec(CB_V), zspec(CB_GR),
            pl.BlockSpec((BATCH, L, 2 * R_W), lambda c: (0, c, CB_WA)),
            pl.BlockSpec((None, N_VEC, D_R), lambda c: (layer, 0, 0)),
            pl.BlockSpec((None, 2 * R_W, 2 * D_R), lambda c: (layer, 0, 0)),
            full(tri), full(pmat), full(gmask),
        ] + extra_specs,
        out_specs=[
            pl.BlockSpec((BATCH, L, D_R), lambda c: (0, c, 0)),
            pl.BlockSpec((None, BATCH, H_R, HEAD, HEAD), lambda c: (layer, 0, 0, 0, 0)),
        ],
        out_shape=[
            jax.ShapeDtypeStruct((BATCH, SEQ, D_R), BF16),
            jax.ShapeDtypeStruct((DEPTH, BATCH, H_R, HEAD, HEAD), F32),
        ],
        scratch_shapes=[
            pltpu.VMEM((BATCH, N_HG, GL, GL), F32),
            row_scr(D_R), row_scr(D_R), row_scr(D_R), row_scr(2 * R_W),
            big, big, big, big, big, big, big, big, big, big,
        ],
        input_output_aliases={n_in: 1} if alias else {},
        compiler_params=_params("arbitrary"),
        name="wkv_prompt",
    )(z3, z3, z3, z3, z3, vecs, w2a2, tri, pmat, gmask, *extra_ops)


def _wkv_sample_kernel(r_ref, k_ref, v_ref, gr_ref, wa_ref, pr_ref, pk_ref, pv_ref, pwa_ref, s_in_ref,
                       vecs_ref, w2a2_ref, tri_ref, pmat_ref, gmask_ref, *rest):
    y_ref, s_out_ref, seg_s, u_s, zd_s = rest[-5:]
    nb, T = SAMPLE_NB, DEC_SEQ
    R = nb * T
    vecs = vecs_ref[...]
    pmat = pmat_ref[...]
    gmask = gmask_ref[...]
    bc = lambda x, j: _bcast_row(x, nb, j)

    r, k, v, wa = r_ref[...], k_ref[...], v_ref[...], wa_ref[...]
    ops = _wkv_prep(r, k, v, wa, _prev_rows(r, pr_ref[...], nb), _prev_rows(k, pk_ref[...], nb),
                    _prev_rows(v, pv_ref[...], nb), _prev_rows(wa, pwa_ref[...], nb),
                    vecs, w2a2_ref[...], tri_ref[...], pmat, lambda x: bc(x, T - 1))
    rt, kt, at, bt, vv = ops["rt"], ops["kt"], ops["at"], ops["bt"], ops["v"]
    zd_s[...] = jnp.concatenate([ops["bd"], ops["kd"]], axis=0)

    for j in range(T):
        btj, ktj = bc(bt, j), bc(kt, j)
        seg = _segsum(jnp.concatenate([at * btj, at * ktj, rt * btj, rt * ktj], axis=0), pmat)
        for q in range(4):
            seg_s[j, q] = seg[q * R:(q + 1) * R]

    ga_rows, gr_rows = [], []
    for b in range(nb):
        ga_g, gr_g = [], []
        for g in range(N_HG):
            sl = slice(g * GL, (g + 1) * GL)
            s4 = s_in_ref[b, g * HG:(g + 1) * HG].reshape(GL, HEAD)
            s_bd = jnp.concatenate([s4.astype(BF16)] * HG, axis=1) * gmask
            lhs = jnp.concatenate([at[b * T:(b + 1) * T, sl], rt[b * T:(b + 1) * T, sl]], axis=0)
            gs = _mm_nt(lhs, s_bd)
            ga_g.append(gs[:T])
            gr_g.append(gs[T:])
        ga_rows.append(jnp.concatenate(ga_g, axis=1))
        gr_rows.append(jnp.concatenate(gr_g, axis=1))
    g_a = jnp.concatenate(ga_rows, axis=0)
    g_r = jnp.concatenate(gr_rows, axis=0)

    t_idx = _iota((R, D_R), 0) & (T - 1)
    u = g_a
    for j in range(T - 1):
        u = u + jnp.where(t_idx > j, seg_s[j, 1] * bc(vv, j), 0.0)
    for j in range(T - 1):
        u = u + jnp.where(t_idx > j, seg_s[j, 0] * bc(u, j), 0.0)
    y = g_r
    for j in range(T):
        y = y + jnp.where(t_idx >= j, seg_s[j, 2] * bc(u, j) + seg_s[j, 3] * bc(vv, j), 0.0)
    y_ref[...] = _wkv_finish(y, ops["bonus"], gr_ref[...], vecs, pmat)

    u_s[...] = jnp.concatenate([u, vv], axis=0)
    wl = ops["wl"]
    row2 = _iota((2 * R, GL), 0) & (R - 1)
    for g in range(N_HG):
        sl = slice(g * GL, (g + 1) * GL)
        uv_g = u_s[:, sl]
        zd_g = zd_s[:, sl]
        for b in range(nb):
            mine = (row2 >= b * T) & (row2 < (b + 1) * T)
            upd = _mm_tn(uv_g, jnp.where(mine, zd_g, 0.0))
            for j in range(HG):
                h = g * HG + j
                s_out_ref[b, h] = (s_in_ref[b, h] * wl[b * T:b * T + 1, h * HEAD:(h + 1) * HEAD]
                                   + upd[j * HEAD:(j + 1) * HEAD, j * HEAD:(j + 1) * HEAD])


def _wkv_sample(z, state_shift, state_wkv, vecs, w2a2, tri, pmat, gmask, s_buf, layer):
    nb, T = SAMPLE_NB, DEC_SEQ
    R = nb * T
    zspec = lambda cb: pl.BlockSpec((R, D_R), lambda i: (i, cb))
    pspec = lambda w, cb: pl.BlockSpec((None, nb, 1, w), lambda i: (layer, i, 0, cb))
    full = lambda a: pl.BlockSpec(a.shape, lambda i: (0,) * a.ndim)
    sspec = pl.BlockSpec((None, nb, H_R, HEAD, HEAD), lambda i: (layer, i, 0, 0, 0))
    extra_specs, extra_ops, alias = _stack_io(s_buf, layer)
    n_in = 15
    return pl.pallas_call(
        _wkv_sample_kernel,
        grid=(DEC_BATCH // nb,),
        in_specs=[
            zspec(CB_R), zspec(CB_K), zspec(CB_V), zspec(CB_GR),
            pl.BlockSpec((R, 2 * R_W), lambda i: (i, CB_WA)),
            pspec(D_R, 0), pspec(D_R, 1), pspec(D_R, 2), pspec(2 * R_W, 3 * D_R // (2 * R_W)),
            sspec,
            pl.BlockSpec((None, N_VEC, D_R), lambda i: (layer, 0, 0)),
            pl.BlockSpec((None, 2 * R_W, 2 * D_R), lambda i: (layer, 0, 0)),
            full(tri), full(pmat), full(gmask),
        ] + extra_specs,
        out_specs=[pl.BlockSpec((R, D_R), lambda i: (i, 0)), sspec],
        out_shape=[
            jax.ShapeDtypeStruct((N_SAMPLE, D_R), BF16),
            jax.ShapeDtypeStruct((DEPTH, DEC_BATCH, H_R, HEAD, HEAD), F32),
        ],
        scratch_shapes=[
            pltpu.VMEM((T, 4, R, D_R), F32),
            pltpu.VMEM((2 * R, D_R), F32),
            pltpu.VMEM((2 * R, D_R), F32),
        ],
        input_output_aliases={n_in: 1} if alias else {},
        compiler_params=_params("arbitrary"),
        name="wkv_sample",
    )(z, z, z, z, z, state_shift, state_shift, state_shift, state_shift, state_wkv,
      vecs, w2a2, tri, pmat, gmask, *extra_ops)


def _layer_norm(x, g, b):
    mu = jnp.mean(x, axis=-1, keepdims=True)
    d = x - mu
    var = jnp.mean(d * d, axis=-1, keepdims=True)
    return d * lax.rsqrt(var + LN_EPS) * g + b


def _sgu_prompt_kernel(u_ref, vg_ref, gg_ref, lng_ref, lnb_ref, w_ref, bias_ref, y_ref):
    L = SGU_L
    vn = _layer_norm(vg_ref[...], lng_ref[...], lnb_ref[...])
    causal = _iota((L, L), 0) >= _iota((L, L), 1)
    parts = []
    for g in range(N_GROUPS):
        wm = jnp.where(causal, w_ref[g], 0.0)
        parts.append(_mm(wm, vn[:, g * GW:(g + 1) * GW]))
    s = jnp.concatenate(parts, axis=1) + bias_ref[...]
    gg = gg_ref[...]
    y_ref[...] = (u_ref[...] * s * (gg * jax.nn.sigmoid(gg))).astype(BF16)


def _sgu_prompt(z, ln_g, ln_b, sgu_w, bias_exp, layer):
    L = SGU_L
    zspec = lambda cb: pl.BlockSpec((L, D_G), lambda i: (i, cb))
    vspec = pl.BlockSpec((None, 1, D_G), lambda i: (layer, 0, 0))
    return pl.pallas_call(
        _sgu_prompt_kernel,
        grid=(N_PROMPT // L,),
        in_specs=[
            zspec(CB_U), zspec(CB_VG), zspec(CB_GG), vspec, vspec,
            pl.BlockSpec((None, N_GROUPS, L, L), lambda i: (layer, 0, 0, 0)),
            pl.BlockSpec((None, L, D_G), lambda i: (layer, 0, 0)),
        ],
        out_specs=pl.BlockSpec((L, D_G), lambda i: (i, 0)),
        out_shape=jax.ShapeDtypeStruct((N_PROMPT, D_G), BF16),
        compiler_params=_params("arbitrary"),
        name="sgu_prompt",
    )(z, z, z, ln_g, ln_b, sgu_w, bias_exp)


def _sgu_sample_kernel(u_ref, vg_ref, gg_ref, lng_ref, lnb_ref, wc_ref, bias_ref, *rest):
    y_ref, vn_ref = rest[-2:]
    T = DEC_SEQ
    R = u_ref.shape[0]
    nb = R // T
    vn = _layer_norm(vg_ref[...], lng_ref[...], lnb_ref[...])
    vn_ref[...] = vn
    vn3 = vn.reshape(nb, T, D_G)
    s3 = jnp.broadcast_to(bias_ref[...][None], (nb, T, D_G))
    for j in range(T):
        s3 = s3 + jnp.broadcast_to(vn3[:, j:j + 1, :], vn3.shape) * wc_ref[j][None]
    gg = gg_ref[...]
    y_ref[...] = (u_ref[...] * s3.reshape(R, D_G) * (gg * jax.nn.sigmoid(gg))).astype(BF16)


def _sgu_sample(z, ln_g, ln_b, wc, bias_c, vn_buf, layer):
    R = 128
    zspec = lambda cb: pl.BlockSpec((R, D_G), lambda i: (i, cb))
    vspec = pl.BlockSpec((None, 1, D_G), lambda i: (layer, 0, 0))
    extra_specs, extra_ops, alias = _stack_io(vn_buf, layer)
    n_in = 7
    return pl.pallas_call(
        _sgu_sample_kernel,
        grid=(N_SAMPLE // R,),
        in_specs=[
            zspec(CB_U), zspec(CB_VG), zspec(CB_GG), vspec, vspec,
            pl.BlockSpec((None, DEC_SEQ, DEC_SEQ, D_G), lambda i: (layer, 0, 0, 0)),
            pl.BlockSpec((None, DEC_SEQ, D_G), lambda i: (layer, 0, 0)),
        ] + extra_specs,
        out_specs=[pl.BlockSpec((R, D_G), lambda i: (i, 0)),
                   pl.BlockSpec((None, R, D_G), lambda i: (layer, i, 0))],
        out_shape=[jax.ShapeDtypeStruct((N_SAMPLE, D_G), BF16),
                   jax.ShapeDtypeStruct((DEPTH, N_SAMPLE, D_G), F32)],
        input_output_aliases={n_in: 1} if alias else {},
        compiler_params=_params("arbitrary"),
        name="sgu_sample",
    )(z, z, z, ln_g, ln_b, wc, bias_c, *extra_ops)


def _outproj_kernel(ya_ref, yb_ref, ga_ref, gb_ref, h_ref, wa_ref, wb_ref, wo_ref, gn_ref, *outs):
    ya = jnp.dot(ya_ref[...], wa_ref[...], preferred_element_type=F32)
    yb = jnp.dot(yb_ref[...], wb_ref[...], preferred_element_type=F32)
    m = jax.nn.sigmoid(ga_ref[...]) * ya + jax.nn.sigmoid(gb_ref[...]) * yb
    h_new = h_ref[...] + jnp.dot(m.astype(BF16), wo_ref[...], preferred_element_type=F32)
    xn = _rms_scale(h_new, gn_ref[...])
    if len(outs) == 1:
        outs[0][...] = xn
    else:
        outs[0][...] = h_new
        outs[1][...] = xn.astype(BF16)


def _outproj(ya_in, yb_in, z, h, wpa, wpb, wout, gn, layer, last):
    n_tok = h.shape[0]
    tm = TM_OUT
    wspec = lambda a: pl.BlockSpec((None,) + a.shape[1:], lambda i: (layer, 0, 0),
                                   pipeline_mode=pl.Buffered(1))
    row = lambda w: pl.BlockSpec((tm, w), lambda i: (i, 0))
    tok = lambda dt: jax.ShapeDtypeStruct((n_tok, D_MODEL), dt)
    return pl.pallas_call(
        _outproj_kernel,
        grid=(n_tok // tm,),
        in_specs=[
            row(D_R), row(D_G), row(D_MODEL),
            pl.BlockSpec((tm, D_MODEL), lambda i: (i, 1)),
            row(D_MODEL),
            wspec(wpa), wspec(wpb), wspec(wout),
            pl.BlockSpec((None, 1, D_MODEL), lambda i: (gn[1], 0, 0)),
        ],
        out_specs=[row(D_MODEL)] if last else [row(D_MODEL), row(D_MODEL)],
        out_shape=[tok(F32)] if last else [tok(F32), tok(BF16)],
        compiler_params=_params("arbitrary"),
        name="outproj",
    )(ya_in, yb_in, z, z, h, wpa, wpb, wout, gn[0])


def _block_tri(n, blk):
    i = jnp.arange(n)
    return ((i[:, None] >= i[None, :]) & (i[:, None] // blk == i[None, :] // blk)).astype(BF16)


def kernel(x_prompt, x_sample, state_wkv, state_shift, norm_g, w_in, shift_mu, w0, w2, a0, a2, k_k, k_a, r_k,
           lnx_g, lnx_b, sgu_ln_g, sgu_ln_b, sgu_w, sgu_b, w_proj_a, w_proj_b, w_out, final_norm_g):
    c_wd, c_gr = 3 * D_R, D_SHIFT
    wpa, wpb, wout = w_proj_a.astype(BF16), w_proj_b.astype(BF16), w_out.astype(BF16)

    zrow = jnp.zeros((DEPTH, D_R), F32)
    mu_wa = jnp.pad(shift_mu[:, c_wd:c_gr], ((0, 0), (0, D_R - 2 * R_W)))
    vec_rows = [shift_mu[:, 0:D_R], shift_mu[:, D_R:2 * D_R], shift_mu[:, 2 * D_R:3 * D_R], w0, a0, k_k, k_a,
                r_k.reshape(DEPTH, D_R), lnx_g, lnx_b, mu_wa] + [zrow] * (N_VEC - 11)
    vecs = jnp.stack(vec_rows, axis=1)
    zblk = jnp.zeros((DEPTH, R_W, D_R), F32)
    w2a2 = jnp.concatenate([jnp.concatenate([w2, zblk], axis=2),
                            jnp.concatenate([zblk, a2], axis=2)], axis=1).astype(BF16)
    norm_g3 = norm_g.reshape(DEPTH, 1, D_MODEL)
    final_g3 = final_norm_g.reshape(1, 1, D_MODEL)
    ln_g3, ln_b3 = sgu_ln_g.reshape(DEPTH, 1, D_G), sgu_ln_b.reshape(DEPTH, 1, D_G)
    bias_exp = jnp.repeat(jnp.swapaxes(sgu_b, 1, 2), GW, axis=2)
    tmask = jnp.tril(jnp.ones((DEC_SEQ, DEC_SEQ), F32))
    wc = jnp.repeat(jnp.transpose(sgu_w[:, :, :DEC_SEQ, :DEC_SEQ] * tmask, (0, 3, 2, 1)), GW, axis=3)
    bias_c = bias_exp[:, :DEC_SEQ, :]

    pmat = ((jnp.arange(GL)[:, None] // HEAD) == (jnp.arange(GL)[None, :] // HEAD)).astype(BF16)
    tri_p = _block_tri(BATCH * WKV_L, WKV_L)
    tri_s = _block_tri(SAMPLE_NB * DEC_SEQ, DEC_SEQ)
    src_cols = jnp.asarray(Z_SRC_UNITS, jnp.int32)

    h_p = x_prompt.reshape(N_PROMPT, D_MODEL)
    h_s = x_sample.reshape(N_SAMPLE, D_MODEL)
    xn_p = _norm_bf16(h_p, norm_g3, 0)
    xn_s = _norm_bf16(h_s, norm_g3, 0)
    wkv_p = wkv_s = vn_s = None
    shift_p, shift_s = [], []
    zs_cols = lambda zz: jnp.concatenate([zz[:, CB_R * D_R:(CB_V + 1) * D_R],
                                          zz[:, CB_WA * LANE:CB_WA * LANE + 2 * R_W]], axis=-1)
    for l in range(DEPTH):
        last = l == DEPTH - 1
        z_p = _inproj(xn_p, w_in, src_cols, l, TM_IN_PROMPT)
        z_s = _inproj(xn_s, w_in, src_cols, l, TM_IN_SAMPLE)
        ya_p, wkv_p = _wkv_prompt(z_p.reshape(BATCH, SEQ, NZ), vecs, w2a2, tri_p, pmat, pmat, wkv_p, l)
        ya_s, wkv_s = _wkv_sample(z_s, state_shift, state_wkv, vecs, w2a2, tri_s, pmat, pmat, wkv_s, l)
        yb_p = _sgu_prompt(z_p, ln_g3, ln_b3, sgu_w, bias_exp, l)
        yb_s, vn_s = _sgu_sample(z_s, ln_g3, ln_b3, wc, bias_c, vn_s, l)
        gn = (final_g3, 0) if last else (norm_g3, l + 1)
        out_p = _outproj(ya_p.reshape(N_PROMPT, D_R), yb_p, z_p, h_p, wpa, wpb, wout, gn, l, last)
        out_s = _outproj(ya_s, yb_s, z_s, h_s, wpa, wpb, wout, gn, l, last)
        if last:
            y_p, y_s = out_p[0], out_s[0]
        else:
            (h_p, xn_p), (h_s, xn_s) = out_p, out_s
        shift_p.append(zs_cols(z_p[SEQ - 1::SEQ])[:, None, :])
        shift_s.append(zs_cols(z_s[DEC_SEQ - 1::DEC_SEQ])[:, None, :])
    return (y_p.reshape(BATCH, SEQ, D_MODEL), y_s.reshape(DEC_BATCH, DEC_SEQ, D_MODEL),
            wkv_p, jnp.stack(shift_p), wkv_s, jnp.stack(shift_s),
            vn_s.reshape(DEPTH, DEC_BATCH, DEC_SEQ, D_G))
```

```python
import math

import jax
import jax.numpy as jnp
from jax import lax
from jax.experimental import pallas as pl
from jax.experimental.pallas import tpu as pltpu

F32 = jnp.float32
BF16 = jnp.bfloat16

D_MODEL = 2048
BATCH = 4
SEQ = 2048
DEPTH = 4
DEC_BATCH = 128
DEC_SEQ = 8
HEAD = 64
D_R = 1024
H_R = 16
R_W = 64
R_A = 64
D_G = 1024
N_GROUPS = 8
GW = 128
D_SHIFT = 3 * D_R + R_W + R_A
RMS_EPS = 1e-6
LN_EPS = 1e-5
GN_EPS = 64e-5
EXP_M05 = math.exp(-0.5)

N_PROMPT = BATCH * SEQ
N_SAMPLE = DEC_BATCH * DEC_SEQ

LANE = 128
CB_U, CB_VG, CB_GG, CB_R, CB_K, CB_V, CB_GR = 4, 5, 6, 7, 8, 9, 10
CB_WA = 88
TN_IN = 512
NZ = 23 * TN_IN
Z_SRC_UNITS = tuple([57 + 4 * j for j in range(8)] + [33 + 4 * j for j in range(6)] + [4 * j for j in range(6)]
                    + [25, 29] + [24])

HG = 4
GL = HG * HEAD
N_HG = H_R // HG

WKV_L = 64
N_CHAIN = BATCH * N_HG
SAMPLE_NB = 8

TM_IN_PROMPT, TM_IN_SAMPLE = 2048, 1024
TM_OUT = 256
SGU_L = 128
SGU_CHUNKS = 4
VMEM_LIMIT = 56 * 1024 * 1024

V_MU_R, V_MU_K, V_MU_V, V_W0, V_A0, V_KK, V_KA, V_RK, V_LNG, V_LNB, V_MU_WA = range(11)
N_VEC = 16


def _dot(a, b):
    return jnp.dot(a, b, preferred_element_type=F32)


def _mm(a, b):
    return _dot(a.astype(BF16), b.astype(BF16))


def _mm_nt(a, b):
    return lax.dot_general(a.astype(BF16), b.astype(BF16), (((1,), (1,)), ((), ())),
                           preferred_element_type=F32)


def _mm_tn(a, b):
    return lax.dot_general(a.astype(BF16), b.astype(BF16), (((0,), (0,)), ((), ())),
                           preferred_element_type=F32)


def _iota(shape, dim):
    return lax.broadcasted_iota(jnp.int32, shape, dim)


def _block_diag(x, gmask_bf):
    reps = GL // x.shape[0]
    return jnp.concatenate([x.astype(BF16)] * reps, axis=0) * gmask_bf


def _segsum(x, pmat):
    rows, n = x.shape[0], x.shape[1] // GL
    xb = x.astype(BF16)
    stacked = jnp.concatenate([xb[:, g * GL:(g + 1) * GL] for g in range(n)], axis=0)
    s = _dot(stacked, pmat)
    return jnp.concatenate([s[g * rows:(g + 1) * rows] for g in range(n)], axis=1)


def _cumsum_rows(tri, x):
    hi = x.astype(BF16)
    lo = (x - hi.astype(F32)).astype(BF16)
    return _dot(tri, hi) + _dot(tri, lo)


def _rms_scale(x, g):
    ms = jnp.mean(x * x, axis=-1, keepdims=True)
    return x * lax.rsqrt(ms + RMS_EPS) * g


def _params(*sem):
    return pltpu.CompilerParams(dimension_semantics=sem, vmem_limit_bytes=VMEM_LIMIT)


def _norm_kernel(x_ref, g_ref, o_ref):
    o_ref[...] = _rms_scale(x_ref[...], g_ref[...]).astype(o_ref.dtype)


def _norm_bf16(x, norm_g3, layer):
    tm = 512
    return pl.pallas_call(
        _norm_kernel,
        grid=(x.shape[0] // tm,),
        in_specs=[pl.BlockSpec((tm, D_MODEL), lambda i: (i, 0)),
                  pl.BlockSpec((None, 1, D_MODEL), lambda i: (layer, 0, 0))],
        out_specs=pl.BlockSpec((tm, D_MODEL), lambda i: (i, 0)),
        out_shape=jax.ShapeDtypeStruct(x.shape, BF16),
        compiler_params=_params("arbitrary"),
        name="norm_in",
    )(x, norm_g3)


def _inproj_kernel(cols_ref, x_ref, w_ref, z_ref):
    del cols_ref
    z_ref[...] = _dot(x_ref[...], w_ref[...].astype(BF16))


def _inproj(xn, w_in, src_cols, layer, tm):
    n_tok = xn.shape[0]
    grid_spec = pltpu.PrefetchScalarGridSpec(
        num_scalar_prefetch=1,
        grid=(n_tok // tm, NZ // TN_IN),
        in_specs=[
            pl.BlockSpec((tm, D_MODEL), lambda i, j, cols: (i, 0)),
            pl.BlockSpec((None, pl.Element(D_MODEL), pl.Element(TN_IN)),
                         lambda i, j, units: (layer, 0, units[j] * LANE)),
        ],
        out_specs=pl.BlockSpec((tm, TN_IN), lambda i, j, cols: (i, j)),
    )
    return pl.pallas_call(
        _inproj_kernel,
        grid_spec=grid_spec,
        out_shape=jax.ShapeDtypeStruct((n_tok, NZ), F32),
        compiler_params=_params("arbitrary", "arbitrary"),
        name="inproj",
    )(src_cols, xn, w_in)


def _shift_mix(x, x_prev, mu):
    return x + (x_prev - x) * mu


def _lowrank_in(xwa):
    return jnp.where(_iota(xwa.shape, 1) < R_W, jnp.tanh(xwa), xwa).astype(BF16)


def _prep_gates(xr, xk, lr_w, lr_a, vecs):
    vec = lambda i: vecs[i:i + 1, :]
    lw = -EXP_M05 * jax.nn.sigmoid(vec(V_W0) + lr_w)
    ag = jax.nn.sigmoid(vec(V_A0) + lr_a)
    kk = xk * vec(V_KK)
    k2 = xk * (1.0 + (ag - 1.0) * vec(V_KA))
    return dict(lw=lw, ag=ag, kk=kk, k2=k2, seg_in=jnp.concatenate([kk * kk, xr * k2 * vec(V_RK)], axis=0))


def _prep_scale(xr, xv, gt, cl, seg, last_row):
    rows = cl.shape[0]
    kkn = gt["kk"] / jnp.maximum(jnp.sqrt(seg[:rows]), 1e-12)
    cl_last = last_row(cl)
    e_neg = jnp.exp(-cl)
    e_rem = jnp.exp(cl_last - cl)
    bb = kkn * gt["ag"]
    return dict(
        rt=xr * jnp.exp(cl),
        kt=gt["k2"] * e_neg,
        at=-kkn * jnp.exp(cl - gt["lw"]),
        bt=bb * e_neg,
        kd=gt["k2"] * e_rem,
        bd=bb * e_rem,
        v=xv,
        wl=jnp.exp(cl_last),
        bonus=seg[rows:] * xv,
    )


def _wkv_prep(r, k, v, wa, r_prev, k_prev, v_prev, wa_prev, vecs, w2a2, tri, pmat, last_row):
    vec = lambda i: vecs[i:i + 1, :]
    xr = _shift_mix(r, r_prev, vec(V_MU_R))
    xk = _shift_mix(k, k_prev, vec(V_MU_K))
    xv = _shift_mix(v, v_prev, vec(V_MU_V))
    lowrank = _dot(_lowrank_in(_shift_mix(wa, wa_prev, vecs[V_MU_WA:V_MU_WA + 1, :2 * R_W])), w2a2)
    gt = _prep_gates(xr, xk, lowrank[:, :D_R], lowrank[:, D_R:], vecs)
    return _prep_scale(xr, xv, gt, _cumsum_rows(tri, gt["lw"]), _segsum(gt["seg_in"], pmat), last_row)


def _wkv_finish(y, bonus, g_r, vecs, pmat):
    mu = _segsum(y, pmat) * (1.0 / HEAD)
    d = y - mu
    var = _segsum(d * d, pmat) * (1.0 / HEAD)
    yn = d * lax.rsqrt(var + GN_EPS) * vecs[V_LNG:V_LNG + 1, :] + vecs[V_LNB:V_LNB + 1, :]
    return ((yn + bonus) * (g_r * jax.nn.sigmoid(g_r))).astype(BF16)


def _grouped(x, nb):
    return x.reshape(nb, x.shape[0] // nb, x.shape[-1])


def _bcast_row(x, nb, j):
    x3 = _grouped(x, nb)
    return jnp.broadcast_to(x3[:, j:j + 1, :], x3.shape).reshape(x.shape)


def _prev_rows(x, prev, nb):
    t = x.shape[0] // nb
    first = (_iota(x.shape, 0) & (t - 1)) == 0
    return jnp.where(first, jnp.broadcast_to(prev, (nb, t, x.shape[-1])).reshape(x.shape), pltpu.roll(x, 1, 0))


def _interleave(major, minor):
    done = 0
    for i, piece in enumerate(major):
        piece()
        upto = (i + 1) * len(minor) // len(major)
        for other in minor[done:upto]:
            other()
        done = upto
    for other in minor[done:]:
        other()


def _chunk_buffers():
    L = WKV_L
    return [
        pltpu.VMEM((N_CHAIN, 2 * L, GL), BF16),
        pltpu.VMEM((N_CHAIN, L, GL), BF16),
        pltpu.VMEM((N_CHAIN, L, GL), BF16),
        pltpu.VMEM((N_CHAIN, L, 2 * GL), BF16),
        pltpu.VMEM((N_CHAIN, 2 * L, GL), BF16),
        pltpu.VMEM((BATCH * L, D_R), F32),
        pltpu.VMEM((BATCH * L, D_R), F32),
        pltpu.VMEM((BATCH, 8, D_R), F32),
    ]


N_CHUNK_BUF = 8


def _wkv_prompt_kernel(r_ref, k_ref, v_ref, wa_ref, gr_ref, vecs_ref, w2a2_ref, tri_ref, pmat_ref, gmask_ref,
                       *rest):
    n_scr = 6 + 2 * N_CHUNK_BUF
    y_ref, s_out_ref, sh_out_ref = rest[-n_scr - 3:-n_scr]
    s_ref, pr_ref, pk_ref, pv_ref, pwa_ref, y_s = rest[-n_scr:-n_scr + 6]
    set_a = rest[-2 * N_CHUNK_BUF:-N_CHUNK_BUF]
    set_b = rest[-N_CHUNK_BUF:]
    prevs = (pr_ref, pk_ref, pv_ref, pwa_ref)
    s = pl.program_id(0)

    @pl.when(s == 0)
    def _():
        s_ref[...] = jnp.zeros_like(s_ref)
        for ref in prevs + tuple(set_b):
            ref[...] = jnp.zeros_like(ref)

    args = (r_ref, k_ref, v_ref, wa_ref, gr_ref, vecs_ref, w2a2_ref, tri_ref, pmat_ref, gmask_ref,
            y_ref, s_ref, prevs, y_s)

    @pl.when((s & 1) == 0)
    def _():
        _wkv_prompt_step(*args, set_a, set_b)

    @pl.when((s & 1) == 1)
    def _():
        _wkv_prompt_step(*args, set_b, set_a)

    @pl.when(s == pl.num_programs(0) - 1)
    def _():
        for b in range(BATCH):
            for g in range(N_HG):
                for j in range(HG):
                    s_out_ref[b, g * HG + j] = s_ref[b, g, j * HEAD:(j + 1) * HEAD, :][:, j * HEAD:(j + 1) * HEAD]
        sh_out_ref[:, :, 0:D_R] = pr_ref[...]
        sh_out_ref[:, :, D_R:2 * D_R] = pk_ref[...]
        sh_out_ref[:, :, 2 * D_R:3 * D_R] = pv_ref[...]
        sh_out_ref[:, :, 3 * D_R:D_SHIFT] = pwa_ref[...]


def _wkv_prompt_step(r_ref, k_ref, v_ref, wa_ref, gr_ref, vecs_ref, w2a2_ref, tri_ref, pmat_ref, gmask_ref,
                     y_ref, s_ref, prevs, y_s, wset, rset):
    L, nb = WKV_L, BATCH
    vecs = vecs_ref[...]
    pmat = pmat_ref[...]
    gmask = gmask_ref[...]
    gmask_f = gmask.astype(F32)
    tri = tri_ref[...]
    rs = lambda b: slice(b * L, (b + 1) * L)
    sl = lambda g: slice(g * GL, (g + 1) * GL)
    chains = [(b, g) for b in range(nb) for g in range(N_HG)]
    lhs_r, p_r, ak_r, ar_r, zd_r, v_r, bonus_r, wl_r = rset
    lhs_w, p_w, ak_w, ar_w, zd_w, v_w, bonus_w, wl_w = wset
    pr_ref, pk_ref, pv_ref, pwa_ref = prevs

    st = {}

    def b_state_terms(n, b, g):
        st["gs", n] = _mm_nt(lhs_r[n], s_ref[b, g])

    def b_rhs(n, b, g):
        st["vbd", n] = _block_diag(v_r[rs(b), sl(g)], gmask)
        st["rhs", n] = st["gs", n][:L] + _dot(ak_r[n], st["vbd", n])

    def b_u(n, b, g):
        st["u", n] = _dot(p_r[n], _block_diag(st["rhs", n], gmask))

    def b_y(n, b, g):
        y_s[rs(b), sl(g)] = st["gs", n][L:] + _dot(
            ar_r[n], jnp.concatenate([_block_diag(st["u", n], gmask), st["vbd", n]], axis=0))

    def b_state(n, b, g):
        uv = jnp.concatenate([st["u", n], v_r[rs(b), sl(g)]], axis=0)
        s_ref[b, g] = (s_ref[b, g] * wl_r[b, 0:1, sl(g)] + _mm_tn(uv, zd_r[n])) * gmask_f

    def staged(fns, members):
        return [(lambda f=f, n=n, b=b, g=g: f(n, b, g)) for f in fns for n, (b, g) in members]

    state_part = staged([b_state_terms, b_rhs, b_u, b_y, b_state], list(enumerate(chains)))

    row = _iota((L, GL), 0)
    col = _iota((L, GL), 1) & (HEAD - 1)
    strict = row > col
    incl2 = jnp.concatenate([row >= col] * 2, axis=1)
    eye = jnp.where(row == col, 1.0, 0.0)
    pp = {}

    def prep_pieces(b):
        def mix_r():
            x = r_ref[b]
            pp["xr", b] = _shift_mix(x, _prev_rows(x, pr_ref[b:b + 1], 1), vecs[V_MU_R:V_MU_R + 1, :])
            pr_ref[b] = x[L - 1:L, :]

        def mix_k():
            x = k_ref[b]
            pp["xk", b] = _shift_mix(x, _prev_rows(x, pk_ref[b:b + 1], 1), vecs[V_MU_K:V_MU_K + 1, :])
            pk_ref[b] = x[L - 1:L, :]

        def mix_v():
            x = v_ref[b]
            xv = _shift_mix(x, _prev_rows(x, pv_ref[b:b + 1], 1), vecs[V_MU_V:V_MU_V + 1, :])
            pp["xv", b] = xv
            v_w[rs(b), :] = xv
            pv_ref[b] = x[L - 1:L, :]

        def lowrank():
            x = wa_ref[b]
            xwa = _shift_mix(x, _prev_rows(x, pwa_ref[b:b + 1], 1), vecs[V_MU_WA:V_MU_WA + 1, :2 * R_W])
            pwa_ref[b] = x[L - 1:L, :]
            pp["lr", b] = _dot(_lowrank_in(xwa), w2a2_ref[...])

        def gates(g):
            def run():
                xr, xk = pp["xr", b][:, sl(g)], pp["xk", b][:, sl(g)]
                lr = pp["lr", b]
                gt = _prep_gates(xr, xk, lr[:, sl(g)], lr[:, D_R + g * GL:D_R + (g + 1) * GL], vecs[:, sl(g)])
                pp["gt", b, g] = gt
                pp["cl", b, g] = _cumsum_rows(tri, gt["lw"])
                pp["seg", b, g] = _segsum(gt["seg_in"], pmat)
            return run

        def scale(g):
            def run():
                n = b * N_HG + g
                ops = _prep_scale(pp["xr", b][:, sl(g)], pp["xv", b][:, sl(g)], pp["gt", b, g], pp["cl", b, g],
                                  pp["seg", b, g], lambda x: jnp.broadcast_to(x[L - 1:L, :], x.shape))
                bonus_w[rs(b), sl(g)] = ops["bonus"]
                wl_w[b, :, sl(g)] = ops["wl"][0:8, :]
                lhs = jnp.concatenate([ops["at"], ops["rt"]], axis=0).astype(BF16)
                lhs_w[n] = lhs
                zd_w[n] = jnp.concatenate([ops["bd"], ops["kd"]], axis=0).astype(BF16)
                pp["lhs", n] = lhs
                pp["bk", n] = jnp.concatenate([_block_diag(ops["bt"], gmask), _block_diag(ops["kt"], gmask)],
                                              axis=0)
            return run

        return ([mix_r, mix_k, mix_v, lowrank] + [gates(g) for g in range(N_HG)]
                + [scale(g) for g in range(N_HG)])

    def a_scores(n, b, g):
        sc = _mm_nt(pp["lhs", n], pp["bk", n])
        a_ab = jnp.where(strict, sc[:L, :GL], 0.0)
        ak_w[n] = jnp.where(strict, sc[:L, GL:], 0.0).astype(BF16)
        ar_w[n] = jnp.where(incl2, sc[L:], 0.0).astype(BF16)
        pp["x", n] = a_ab.astype(BF16)
        pp["xbd", n] = _block_diag(a_ab, gmask)
        pp["p", n] = eye + a_ab

    def a_square(n, b, g):
        x = _dot(pp["x", n], pp["xbd", n]).astype(BF16)
        pp["x", n], pp["xbd", n] = x, _block_diag(x, gmask)

    def a_double(n, b, g):
        xp = _dot(jnp.concatenate([pp["x", n], pp["p", n].astype(BF16)], axis=0), pp["xbd", n])
        x = xp[:L].astype(BF16)
        pp["x", n], pp["xbd", n] = x, _block_diag(x, gmask)
        pp["p", n] = pp["p", n] + xp[L:]

    def a_inverse(n, b, g):
        p_w[n] = (pp["p", n] + _dot(pp["p", n].astype(BF16), pp["xbd", n])).astype(BF16)

    inv_stages = [a_scores, a_square] + [a_double] * (int(math.log2(L)) - 2) + [a_inverse]
    half = N_CHAIN // 2
    members = list(enumerate(chains))

    fin = {}

    def fin_mean(g):
        def run():
            y = y_s[:, sl(g)]
            fin[g] = y - _segsum(y, pmat) * (1.0 / HEAD)
        return run

    def fin_out(g):
        def run():
            d = fin[g]
            var = _segsum(d * d, pmat) * (1.0 / HEAD)
            yn = d * lax.rsqrt(var + GN_EPS) * vecs[V_LNG:V_LNG + 1, sl(g)] + vecs[V_LNB:V_LNB + 1, sl(g)]
            g_r = gr_ref[:, :, sl(g)].reshape(nb * L, GL)
            out = (yn + bonus_r[:, sl(g)]) * (g_r * jax.nn.sigmoid(g_r))
            y_ref[:, :, sl(g)] = out.astype(BF16).reshape(nb, L, GL)
        return run

    _interleave(state_part, prep_pieces(0) + prep_pieces(1))
    _interleave(staged(inv_stages, members[:half]), prep_pieces(2) + prep_pieces(3))
    _interleave(staged(inv_stages, members[half:]),
                [fin_mean(g) for g in range(N_HG)] + [fin_out(g) for g in range(N_HG)])


def _stack_io(bufs, n_in, first_out):
    if bufs is None:
        return [], [], {}
    return ([pl.BlockSpec(memory_space=pl.ANY)] * len(bufs), list(bufs),
            {n_in + i: first_out + i for i in range(len(bufs))})


def _wkv_prompt(z3, vecs, w2a2, tri, pmat, gmask, stacked, layer):
    L = WKV_L
    n_chunks = SEQ // L
    cur = lambda s: jnp.minimum(s, n_chunks - 1)
    prv = lambda s: jnp.maximum(s - 1, 0)
    zspec = lambda cb: pl.BlockSpec((BATCH, L, D_R), lambda s: (0, cur(s), cb))
    full = lambda a: pl.BlockSpec(a.shape, lambda s: (0,) * a.ndim)
    row_scr = lambda w: pltpu.VMEM((BATCH, 1, w), F32)
    extra_specs, extra_ops, aliases = _stack_io(stacked, 10, 1)
    return pl.pallas_call(
        _wkv_prompt_kernel,
        grid=(n_chunks + 1,),
        in_specs=[
            zspec(CB_R), zspec(CB_K), zspec(CB_V),
            pl.BlockSpec((BATCH, L, 2 * R_W), lambda s: (0, cur(s), CB_WA)),
            pl.BlockSpec((BATCH, L, D_R), lambda s: (0, prv(s), CB_GR)),
            pl.BlockSpec((None, N_VEC, D_R), lambda s: (layer, 0, 0)),
            pl.BlockSpec((None, 2 * R_W, 2 * D_R), lambda s: (layer, 0, 0)),
            full(tri), full(pmat), full(gmask),
        ] + extra_specs,
        out_specs=[
            pl.BlockSpec((BATCH, L, D_R), lambda s: (0, prv(s), 0)),
            pl.BlockSpec((None, BATCH, H_R, HEAD, HEAD), lambda s: (layer, 0, 0, 0, 0)),
            pl.BlockSpec((None, BATCH, 1, D_SHIFT), lambda s: (layer, 0, 0, 0)),
        ],
        out_shape=[
            jax.ShapeDtypeStruct((BATCH, SEQ, D_R), BF16),
            jax.ShapeDtypeStruct((DEPTH, BATCH, H_R, HEAD, HEAD), F32),
            jax.ShapeDtypeStruct((DEPTH, BATCH, 1, D_SHIFT), F32),
        ],
        scratch_shapes=[
            pltpu.VMEM((BATCH, N_HG, GL, GL), F32),
            row_scr(D_R), row_scr(D_R), row_scr(D_R), row_scr(2 * R_W),
            pltpu.VMEM((BATCH * L, D_R), F32),
        ] + _chunk_buffers() + _chunk_buffers(),
        input_output_aliases=aliases,
        compiler_params=_params("arbitrary"),
        name="wkv_prompt",
    )(z3, z3, z3, z3, z3, vecs, w2a2, tri, pmat, gmask, *extra_ops)


def _wkv_sample_kernel(r_ref, k_ref, v_ref, gr_ref, wa_ref, pr_ref, pk_ref, pv_ref, pwa_ref, s_in_ref,
                       vecs_ref, w2a2_ref, tri_ref, pmat_ref, gmask_ref, *rest):
    y_ref, s_out_ref, sh_out_ref, seg_s, u_s, zd_s = rest[-6:]
    nb, T = SAMPLE_NB, DEC_SEQ
    R = nb * T
    vecs = vecs_ref[...]
    pmat = pmat_ref[...]
    gmask = gmask_ref[...]
    bc = lambda x, j: _bcast_row(x, nb, j)

    r, k, v, wa = r_ref[...], k_ref[...], v_ref[...], wa_ref[...]
    last = lambda x: _grouped(x, nb)[:, T - 1:T, :]
    sh_out_ref[:, :, 0:D_R] = last(r)
    sh_out_ref[:, :, D_R:2 * D_R] = last(k)
    sh_out_ref[:, :, 2 * D_R:3 * D_R] = last(v)
    sh_out_ref[:, :, 3 * D_R:D_SHIFT] = last(wa)
    ops = _wkv_prep(r, k, v, wa, _prev_rows(r, pr_ref[...], nb), _prev_rows(k, pk_ref[...], nb),
                    _prev_rows(v, pv_ref[...], nb), _prev_rows(wa, pwa_ref[...], nb),
                    vecs, w2a2_ref[...], tri_ref[...], pmat, lambda x: bc(x, T - 1))
    rt, kt, at, bt, vv = ops["rt"], ops["kt"], ops["at"], ops["bt"], ops["v"]
    zd_s[...] = jnp.concatenate([ops["bd"], ops["kd"]], axis=0)

    for j in range(T):
        btj, ktj = bc(bt, j), bc(kt, j)
        seg = _segsum(jnp.concatenate([at * btj, at * ktj, rt * btj, rt * ktj], axis=0), pmat)
        for q in range(4):
            seg_s[j, q] = seg[q * R:(q + 1) * R]

    ga_rows, gr_rows = [], []
    for b in range(nb):
        ga_g, gr_g = [], []
        for g in range(N_HG):
            sl = slice(g * GL, (g + 1) * GL)
            s4 = s_in_ref[b, g * HG:(g + 1) * HG].reshape(GL, HEAD)
            s_bd = jnp.concatenate([s4.astype(BF16)] * HG, axis=1) * gmask
            lhs = jnp.concatenate([at[b * T:(b + 1) * T, sl], rt[b * T:(b + 1) * T, sl]], axis=0)
            gs = _mm_nt(lhs, s_bd)
            ga_g.append(gs[:T])
            gr_g.append(gs[T:])
        ga_rows.append(jnp.concatenate(ga_g, axis=1))
        gr_rows.append(jnp.concatenate(gr_g, axis=1))
    g_a = jnp.concatenate(ga_rows, axis=0)
    g_r = jnp.concatenate(gr_rows, axis=0)

    t_idx = _iota((R, D_R), 0) & (T - 1)
    u = g_a
    for j in range(T - 1):
        u = u + jnp.where(t_idx > j, seg_s[j, 1] * bc(vv, j), 0.0)
    for j in range(T - 1):
        u = u + jnp.where(t_idx > j, seg_s[j, 0] * bc(u, j), 0.0)
    y = g_r
    for j in range(T):
        y = y + jnp.where(t_idx >= j, seg_s[j, 2] * bc(u, j) + seg_s[j, 3] * bc(vv, j), 0.0)
    y_ref[...] = _wkv_finish(y, ops["bonus"], gr_ref[...], vecs, pmat)

    u_s[...] = jnp.concatenate([u, vv], axis=0)
    wl = ops["wl"]
    row2 = _iota((2 * R, GL), 0) & (R - 1)
    for g in range(N_HG):
        sl = slice(g * GL, (g + 1) * GL)
        uv_g = u_s[:, sl]
        zd_g = zd_s[:, sl]
        for b in range(nb):
            mine = (row2 >= b * T) & (row2 < (b + 1) * T)
            upd = _mm_tn(uv_g, jnp.where(mine, zd_g, 0.0))
            for j in range(HG):
                h = g * HG + j
                s_out_ref[b, h] = (s_in_ref[b, h] * wl[b * T:b * T + 1, h * HEAD:(h + 1) * HEAD]
                                   + upd[j * HEAD:(j + 1) * HEAD, j * HEAD:(j + 1) * HEAD])


def _wkv_sample(z, state_shift, state_wkv, vecs, w2a2, tri, pmat, gmask, stacked, layer):
    nb, T = SAMPLE_NB, DEC_SEQ
    R = nb * T
    zspec = lambda cb: pl.BlockSpec((R, D_R), lambda i: (i, cb))
    pspec = lambda w, cb: pl.BlockSpec((None, nb, 1, w), lambda i: (layer, i, 0, cb))
    full = lambda a: pl.BlockSpec(a.shape, lambda i: (0,) * a.ndim)
    sspec = pl.BlockSpec((None, nb, H_R, HEAD, HEAD), lambda i: (layer, i, 0, 0, 0))
    extra_specs, extra_ops, aliases = _stack_io(stacked, 15, 1)
    return pl.pallas_call(
        _wkv_sample_kernel,
        grid=(DEC_BATCH // nb,),
        in_specs=[
            zspec(CB_R), zspec(CB_K), zspec(CB_V), zspec(CB_GR),
            pl.BlockSpec((R, 2 * R_W), lambda i: (i, CB_WA)),
            pspec(D_R, 0), pspec(D_R, 1), pspec(D_R, 2), pspec(2 * R_W, 3 * D_R // (2 * R_W)),
            sspec,
            pl.BlockSpec((None, N_VEC, D_R), lambda i: (layer, 0, 0)),
            pl.BlockSpec((None, 2 * R_W, 2 * D_R), lambda i: (layer, 0, 0)),
            full(tri), full(pmat), full(gmask),
        ] + extra_specs,
        out_specs=[pl.BlockSpec((R, D_R), lambda i: (i, 0)), sspec,
                   pl.BlockSpec((None, nb, 1, D_SHIFT), lambda i: (layer, i, 0, 0))],
        out_shape=[
            jax.ShapeDtypeStruct((N_SAMPLE, D_R), BF16),
            jax.ShapeDtypeStruct((DEPTH, DEC_BATCH, H_R, HEAD, HEAD), F32),
            jax.ShapeDtypeStruct((DEPTH, DEC_BATCH, 1, D_SHIFT), F32),
        ],
        scratch_shapes=[
            pltpu.VMEM((T, 4, R, D_R), F32),
            pltpu.VMEM((2 * R, D_R), F32),
            pltpu.VMEM((2 * R, D_R), F32),
        ],
        input_output_aliases=aliases,
        compiler_params=_params("arbitrary"),
        name="wkv_sample",
    )(z, z, z, z, z, state_shift, state_shift, state_shift, state_shift, state_wkv,
      vecs, w2a2, tri, pmat, gmask, *extra_ops)


def _layer_norm(x, g, b):
    mu = jnp.mean(x, axis=-1, keepdims=True)
    d = x - mu
    var = jnp.mean(d * d, axis=-1, keepdims=True)
    return d * lax.rsqrt(var + LN_EPS) * g + b


def _sgu_prompt_kernel(u_ref, vg_ref, gg_ref, lng_ref, lnb_ref, w_ref, bias_ref, y_ref):
    L = SGU_L
    vn = _layer_norm(vg_ref[...], lng_ref[...], lnb_ref[...]).astype(BF16)
    causal = _iota((L, L), 0) >= _iota((L, L), 1)
    wm = [jnp.where(causal, w_ref[g], 0.0).astype(BF16) for g in range(N_GROUPS)]
    rows = []
    for c in range(SGU_CHUNKS):
        parts = [_dot(wm[g], vn[c * L:(c + 1) * L, g * GW:(g + 1) * GW]) for g in range(N_GROUPS)]
        rows.append(jnp.concatenate(parts, axis=1) + bias_ref[...])
    s = jnp.concatenate(rows, axis=0)
    gg = gg_ref[...]
    y_ref[...] = (u_ref[...] * s * (gg * jax.nn.sigmoid(gg))).astype(BF16)


def _sgu_prompt(z, ln_g, ln_b, sgu_w, bias_exp, layer):
    L = SGU_L
    rows = L * SGU_CHUNKS
    zspec = lambda cb: pl.BlockSpec((rows, D_G), lambda i: (i, cb))
    vspec = pl.BlockSpec((None, 1, D_G), lambda i: (layer, 0, 0))
    return pl.pallas_call(
        _sgu_prompt_kernel,
        grid=(N_PROMPT // rows,),
        in_specs=[
            zspec(CB_U), zspec(CB_VG), zspec(CB_GG), vspec, vspec,
            pl.BlockSpec((None, N_GROUPS, L, L), lambda i: (layer, 0, 0, 0)),
            pl.BlockSpec((None, L, D_G), lambda i: (layer, 0, 0)),
        ],
        out_specs=pl.BlockSpec((rows, D_G), lambda i: (i, 0)),
        out_shape=jax.ShapeDtypeStruct((N_PROMPT, D_G), BF16),
        compiler_params=_params("arbitrary"),
        name="sgu_prompt",
    )(z, z, z, ln_g, ln_b, sgu_w, bias_exp)


def _sgu_sample_kernel(u_ref, vg_ref, gg_ref, lng_ref, lnb_ref, wc_ref, bias_ref, *rest):
    y_ref, vn_ref = rest[-2:]
    T = DEC_SEQ
    R = u_ref.shape[0]
    nb = R // T
    vn = _layer_norm(vg_ref[...], lng_ref[...], lnb_ref[...])
    vn_ref[...] = vn
    vn3 = vn.reshape(nb, T, D_G)
    s3 = jnp.broadcast_to(bias_ref[...][None], (nb, T, D_G))
    for j in range(T):
        s3 = s3 + jnp.broadcast_to(vn3[:, j:j + 1, :], vn3.shape) * wc_ref[j][None]
    gg = gg_ref[...]
    y_ref[...] = (u_ref[...] * s3.reshape(R, D_G) * (gg * jax.nn.sigmoid(gg))).astype(BF16)


def _sgu_sample(z, ln_g, ln_b, wc, bias_c, vn_buf, layer):
    R = 128
    zspec = lambda cb: pl.BlockSpec((R, D_G), lambda i: (i, cb))
    vspec = pl.BlockSpec((None, 1, D_G), lambda i: (layer, 0, 0))
    extra_specs, extra_ops, aliases = _stack_io(None if vn_buf is None else (vn_buf,), 7, 1)
    return pl.pallas_call(
        _sgu_sample_kernel,
        grid=(N_SAMPLE // R,),
        in_specs=[
            zspec(CB_U), zspec(CB_VG), zspec(CB_GG), vspec, vspec,
            pl.BlockSpec((None, DEC_SEQ, DEC_SEQ, D_G), lambda i: (layer, 0, 0, 0)),
            pl.BlockSpec((None, DEC_SEQ, D_G), lambda i: (layer, 0, 0)),
        ] + extra_specs,
        out_specs=[pl.BlockSpec((R, D_G), lambda i: (i, 0)),
                   pl.BlockSpec((None, R, D_G), lambda i: (layer, i, 0))],
        out_shape=[jax.ShapeDtypeStruct((N_SAMPLE, D_G), BF16),
                   jax.ShapeDtypeStruct((DEPTH, N_SAMPLE, D_G), F32)],
        input_output_aliases=aliases,
        compiler_params=_params("arbitrary"),
        name="sgu_sample",
    )(z, z, z, ln_g, ln_b, wc, bias_c, *extra_ops)


def _outproj_kernel(ya_ref, yb_ref, ga_ref, gb_ref, h_ref, wa_ref, wb_ref, wo_ref, gn_ref, *outs):
    ya = _dot(ya_ref[...], wa_ref[...])
    yb = _dot(yb_ref[...], wb_ref[...])
    m = jax.nn.sigmoid(ga_ref[...]) * ya + jax.nn.sigmoid(gb_ref[...]) * yb
    h_new = h_ref[...] + _dot(m.astype(BF16), wo_ref[...])
    xn = _rms_scale(h_new, gn_ref[...])
    if len(outs) == 1:
        outs[0][...] = xn
    else:
        outs[0][...] = h_new
        outs[1][...] = xn.astype(BF16)


def _outproj(ya_in, yb_in, z, h, wpa, wpb, wout, gn, layer, last):
    n_tok = h.shape[0]
    tm = TM_OUT
    wspec = lambda a: pl.BlockSpec((None,) + a.shape[1:], lambda i: (layer, 0, 0),
                                   pipeline_mode=pl.Buffered(1))
    row = lambda w: pl.BlockSpec((tm, w), lambda i: (i, 0))
    tok = lambda dt: jax.ShapeDtypeStruct((n_tok, D_MODEL), dt)
    return pl.pallas_call(
        _outproj_kernel,
        grid=(n_tok // tm,),
        in_specs=[
            row(D_R), row(D_G), row(D_MODEL),
            pl.BlockSpec((tm, D_MODEL), lambda i: (i, 1)),
            row(D_MODEL),
            wspec(wpa), wspec(wpb), wspec(wout),
            pl.BlockSpec((None, 1, D_MODEL), lambda i: (gn[1], 0, 0)),
        ],
        out_specs=[row(D_MODEL)] if last else [row(D_MODEL), row(D_MODEL)],
        out_shape=[tok(F32)] if last else [tok(F32), tok(BF16)],
        compiler_params=_params("arbitrary"),
        name="outproj",
    )(ya_in, yb_in, z, z, h, wpa, wpb, wout, gn[0])


def _block_tri(n, blk):
    i = jnp.arange(n)
    return ((i[:, None] >= i[None, :]) & (i[:, None] // blk == i[None, :] // blk)).astype(BF16)


def kernel(x_prompt, x_sample, state_wkv, state_shift, norm_g, w_in, shift_mu, w0, w2, a0, a2, k_k, k_a, r_k,
           lnx_g, lnx_b, sgu_ln_g, sgu_ln_b, sgu_w, sgu_b, w_proj_a, w_proj_b, w_out, final_norm_g):
    c_wd, c_gr = 3 * D_R, D_SHIFT
    wpa, wpb, wout = w_proj_a.astype(BF16), w_proj_b.astype(BF16), w_out.astype(BF16)

    zrow = jnp.zeros((DEPTH, D_R), F32)
    mu_wa = jnp.pad(shift_mu[:, c_wd:c_gr], ((0, 0), (0, D_R - 2 * R_W)))
    vec_rows = [shift_mu[:, 0:D_R], shift_mu[:, D_R:2 * D_R], shift_mu[:, 2 * D_R:3 * D_R], w0, a0, k_k, k_a,
                r_k.reshape(DEPTH, D_R), lnx_g, lnx_b, mu_wa] + [zrow] * (N_VEC - 11)
    vecs = jnp.stack(vec_rows, axis=1)
    zblk = jnp.zeros((DEPTH, R_W, D_R), F32)
    w2a2 = jnp.concatenate([jnp.concatenate([w2, zblk], axis=2),
                            jnp.concatenate([zblk, a2], axis=2)], axis=1).astype(BF16)
    norm_g3 = norm_g.reshape(DEPTH, 1, D_MODEL)
    final_g3 = final_norm_g.reshape(1, 1, D_MODEL)
    ln_g3, ln_b3 = sgu_ln_g.reshape(DEPTH, 1, D_G), sgu_ln_b.reshape(DEPTH, 1, D_G)
    bias_exp = jnp.repeat(jnp.swapaxes(sgu_b, 1, 2), GW, axis=2)
    tmask = jnp.tril(jnp.ones((DEC_SEQ, DEC_SEQ), F32))
    wc = jnp.repeat(jnp.transpose(sgu_w[:, :, :DEC_SEQ, :DEC_SEQ] * tmask, (0, 3, 2, 1)), GW, axis=3)
    bias_c = bias_exp[:, :DEC_SEQ, :]

    pmat = ((jnp.arange(GL)[:, None] // HEAD) == (jnp.arange(GL)[None, :] // HEAD)).astype(BF16)
    tri_p = _block_tri(WKV_L, WKV_L)
    tri_s = _block_tri(SAMPLE_NB * DEC_SEQ, DEC_SEQ)
    src_cols = jnp.asarray(Z_SRC_UNITS, jnp.int32)

    h_p = x_prompt.reshape(N_PROMPT, D_MODEL)
    h_s = x_sample.reshape(N_SAMPLE, D_MODEL)
    xn_p = _norm_bf16(h_p, norm_g3, 0)
    xn_s = _norm_bf16(h_s, norm_g3, 0)
    stk_p = stk_s = vn_s = None
    for l in range(DEPTH):
        last = l == DEPTH - 1
        z_p = _inproj(xn_p, w_in, src_cols, l, TM_IN_PROMPT)
        z_s = _inproj(xn_s, w_in, src_cols, l, TM_IN_SAMPLE)
        ya_p, *stk_p = _wkv_prompt(z_p.reshape(BATCH, SEQ, NZ), vecs, w2a2, tri_p, pmat, pmat, stk_p, l)
        ya_s, *stk_s = _wkv_sample(z_s, state_shift, state_wkv, vecs, w2a2, tri_s, pmat, pmat, stk_s, l)
        yb_p = _sgu_prompt(z_p, ln_g3, ln_b3, sgu_w, bias_exp, l)
        yb_s, vn_s = _sgu_sample(z_s, ln_g3, ln_b3, wc, bias_c, vn_s, l)
        gn = (final_g3, 0) if last else (norm_g3, l + 1)
        out_p = _outproj(ya_p.reshape(N_PROMPT, D_R), yb_p, z_p, h_p, wpa, wpb, wout, gn, l, last)
        out_s = _outproj(ya_s, yb_s, z_s, h_s, wpa, wpb, wout, gn, l, last)
        if last:
            y_p, y_s = out_p[0], out_s[0]
        else:
            (h_p, xn_p), (h_s, xn_s) = out_p, out_s
    return (y_p.reshape(BATCH, SEQ, D_MODEL), y_s.reshape(DEC_BATCH, DEC_SEQ, D_MODEL),
            stk_p[0], stk_p[1], stk_s[0], stk_s[1], vn_s.reshape(DEPTH, DEC_BATCH, DEC_SEQ, D_G))
```

```python
import math

import jax
import jax.numpy as jnp
from jax import lax
from jax.experimental import pallas as pl
from jax.experimental.pallas import tpu as pltpu

F32 = jnp.float32
BF16 = jnp.bfloat16

D_MODEL = 2048
BATCH = 4
SEQ = 2048
DEPTH = 4
DEC_BATCH = 128
DEC_SEQ = 8
HEAD = 64
D_R = 1024
H_R = 16
R_W = 64
R_A = 64
D_G = 1024
N_GROUPS = 8
GW = 128
D_SHIFT = 3 * D_R + R_W + R_A
RMS_EPS = 1e-6
LN_EPS = 1e-5
GN_EPS = 64e-5
EXP_M05 = math.exp(-0.5)

N_PROMPT = BATCH * SEQ
N_SAMPLE = DEC_BATCH * DEC_SEQ

LANE = 128
CB_U, CB_VG, CB_GG, CB_R, CB_K, CB_V, CB_GR = 4, 5, 6, 7, 8, 9, 10
CB_WA = 88
TN_IN = 512
NZ = 23 * TN_IN
Z_SRC_UNITS = tuple([57 + 4 * j for j in range(8)] + [33 + 4 * j for j in range(6)] + [4 * j for j in range(6)]
                    + [25, 29] + [24])

HG = 4
GL = HG * HEAD
N_HG = H_R // HG

WKV_L = 64
N_CHAIN = BATCH * N_HG
SAMPLE_NB = 8

TM_IN_PROMPT, TM_IN_SAMPLE = 2048, 1024
TM_OUT = 256
SGU_L = 128
SGU_CHUNKS = 4
VMEM_LIMIT = 56 * 1024 * 1024

V_MU_R, V_MU_K, V_MU_V, V_W0, V_A0, V_KK, V_KA, V_RK, V_LNG, V_LNB, V_MU_WA = range(11)
N_VEC = 16


def _dot(a, b):
    return jnp.dot(a, b, preferred_element_type=F32)


def _mm(a, b):
    return _dot(a.astype(BF16), b.astype(BF16))


def _mm_nt(a, b):
    return lax.dot_general(a.astype(BF16), b.astype(BF16), (((1,), (1,)), ((), ())),
                           preferred_element_type=F32)


def _mm_tn(a, b):
    return lax.dot_general(a.astype(BF16), b.astype(BF16), (((0,), (0,)), ((), ())),
                           preferred_element_type=F32)


def _iota(shape, dim):
    return lax.broadcasted_iota(jnp.int32, shape, dim)


def _block_diag(x, gmask_bf):
    reps = GL // x.shape[0]
    return jnp.concatenate([x.astype(BF16)] * reps, axis=0) * gmask_bf


def _segsum(x, pmat):
    rows, n = x.shape[0], x.shape[1] // GL
    xb = x.astype(BF16)
    stacked = jnp.concatenate([xb[:, g * GL:(g + 1) * GL] for g in range(n)], axis=0)
    s = _dot(stacked, pmat)
    return jnp.concatenate([s[g * rows:(g + 1) * rows] for g in range(n)], axis=1)


def _cumsum_rows(tri, x):
    hi = x.astype(BF16)
    lo = (x - hi.astype(F32)).astype(BF16)
    return _dot(tri, hi) + _dot(tri, lo)


def _rms_scale(x, g):
    ms = jnp.mean(x * x, axis=-1, keepdims=True)
    return x * lax.rsqrt(ms + RMS_EPS) * g


def _params(*sem):
    return pltpu.CompilerParams(dimension_semantics=sem, vmem_limit_bytes=VMEM_LIMIT)


def _norm_kernel(x_ref, g_ref, o_ref):
    o_ref[...] = _rms_scale(x_ref[...], g_ref[...]).astype(o_ref.dtype)


def _norm_bf16(x, norm_g3, layer):
    tm = 1024
    return pl.pallas_call(
        _norm_kernel,
        grid=(x.shape[0] // tm,),
        in_specs=[pl.BlockSpec((tm, D_MODEL), lambda i: (i, 0)),
                  pl.BlockSpec((None, 1, D_MODEL), lambda i: (layer, 0, 0))],
        out_specs=pl.BlockSpec((tm, D_MODEL), lambda i: (i, 0)),
        out_shape=jax.ShapeDtypeStruct(x.shape, BF16),
        compiler_params=_params("arbitrary"),
        name="norm_in",
    )(x, norm_g3)


def _inproj_kernel(cols_ref, x_ref, w_ref, z_ref):
    del cols_ref
    z_ref[...] = _dot(x_ref[...], w_ref[...].astype(BF16)).astype(z_ref.dtype)


def _inproj(xn, w_in, src_cols, layer, tm):
    n_tok = xn.shape[0]
    grid_spec = pltpu.PrefetchScalarGridSpec(
        num_scalar_prefetch=1,
        grid=(n_tok // tm, NZ // TN_IN),
        in_specs=[
            pl.BlockSpec((tm, D_MODEL), lambda i, j, cols: (i, 0)),
            pl.BlockSpec((None, pl.Element(D_MODEL), pl.Element(TN_IN)),
                         lambda i, j, units: (layer, 0, units[j] * LANE)),
        ],
        out_specs=pl.BlockSpec((tm, TN_IN), lambda i, j, cols: (i, j)),
    )
    return pl.pallas_call(
        _inproj_kernel,
        grid_spec=grid_spec,
        out_shape=jax.ShapeDtypeStruct((n_tok, NZ), BF16),
        compiler_params=_params("arbitrary", "arbitrary"),
        name="inproj",
    )(src_cols, xn, w_in)


def _shift_mix(x, x_prev, mu):
    return x + (x_prev - x) * mu


def _lowrank_in(xwa):
    return jnp.where(_iota(xwa.shape, 1) < R_W, jnp.tanh(xwa), xwa).astype(BF16)


def _prep_gates(xr, xk, lr_w, lr_a, vecs):
    vec = lambda i: vecs[i:i + 1, :]
    lw = -EXP_M05 * jax.nn.sigmoid(vec(V_W0) + lr_w)
    ag = jax.nn.sigmoid(vec(V_A0) + lr_a)
    kk = xk * vec(V_KK)
    k2 = xk * (1.0 + (ag - 1.0) * vec(V_KA))
    return dict(lw=lw, ag=ag, kk=kk, k2=k2, seg_in=jnp.concatenate([kk * kk, xr * k2 * vec(V_RK)], axis=0))


def _prep_scale(xr, xv, gt, cl, seg, last_row):
    rows = cl.shape[0]
    kkn = gt["kk"] / jnp.maximum(jnp.sqrt(seg[:rows]), 1e-12)
    cl_last = last_row(cl)
    e_neg = jnp.exp(-cl)
    e_rem = jnp.exp(cl_last - cl)
    bb = kkn * gt["ag"]
    return dict(
        rt=xr * jnp.exp(cl),
        kt=gt["k2"] * e_neg,
        at=-kkn * jnp.exp(cl - gt["lw"]),
        bt=bb * e_neg,
        kd=gt["k2"] * e_rem,
        bd=bb * e_rem,
        v=xv,
        wl=jnp.exp(cl_last),
        bonus=seg[rows:] * xv,
    )


def _wkv_prep(r, k, v, wa, r_prev, k_prev, v_prev, wa_prev, vecs, w2a2, tri, pmat, last_row):
    vec = lambda i: vecs[i:i + 1, :]
    xr = _shift_mix(r, r_prev, vec(V_MU_R))
    xk = _shift_mix(k, k_prev, vec(V_MU_K))
    xv = _shift_mix(v, v_prev, vec(V_MU_V))
    lowrank = _dot(_lowrank_in(_shift_mix(wa, wa_prev, vecs[V_MU_WA:V_MU_WA + 1, :2 * R_W])), w2a2)
    gt = _prep_gates(xr, xk, lowrank[:, :D_R], lowrank[:, D_R:], vecs)
    return _prep_scale(xr, xv, gt, _cumsum_rows(tri, gt["lw"]), _segsum(gt["seg_in"], pmat), last_row)


def _wkv_finish(y, bonus, g_r, vecs, pmat):
    mu = _segsum(y, pmat) * (1.0 / HEAD)
    d = y - mu
    var = _segsum(d * d, pmat) * (1.0 / HEAD)
    yn = d * lax.rsqrt(var + GN_EPS) * vecs[V_LNG:V_LNG + 1, :] + vecs[V_LNB:V_LNB + 1, :]
    return ((yn + bonus) * (g_r * jax.nn.sigmoid(g_r))).astype(BF16)


def _grouped(x, nb):
    return x.reshape(nb, x.shape[0] // nb, x.shape[-1])


def _bcast_row(x, nb, j):
    x3 = _grouped(x, nb)
    return jnp.broadcast_to(x3[:, j:j + 1, :], x3.shape).reshape(x.shape)


def _prev_rows(x, prev, nb):
    t = x.shape[0] // nb
    first = (_iota(x.shape, 0) & (t - 1)) == 0
    return jnp.where(first, jnp.broadcast_to(prev, (nb, t, x.shape[-1])).reshape(x.shape), pltpu.roll(x, 1, 0))


def _interleave(major, minor):
    done = 0
    for i, piece in enumerate(major):
        piece()
        upto = (i + 1) * len(minor) // len(major)
        for other in minor[done:upto]:
            other()
        done = upto
    for other in minor[done:]:
        other()


def _chunk_buffers():
    L = WKV_L
    return [
        pltpu.VMEM((N_CHAIN, 2 * L, GL), BF16),
        pltpu.VMEM((N_CHAIN, L, GL), BF16),
        pltpu.VMEM((N_CHAIN, L, GL), BF16),
        pltpu.VMEM((N_CHAIN, L, 2 * GL), BF16),
        pltpu.VMEM((N_CHAIN, 2 * L, GL), BF16),
        pltpu.VMEM((BATCH * L, D_R), F32),
        pltpu.VMEM((BATCH * L, D_R), F32),
        pltpu.VMEM((BATCH, 8, D_R), F32),
    ]


N_CHUNK_BUF = 8


def _wkv_prompt_kernel(r_ref, k_ref, v_ref, wa_ref, gr_ref, vecs_ref, w2a2_ref, tri_ref, pmat_ref, gmask_ref,
                       *rest):
    n_scr = 6 + 2 * N_CHUNK_BUF
    y_ref, s_out_ref, sh_out_ref = rest[-n_scr - 3:-n_scr]
    s_ref, pr_ref, pk_ref, pv_ref, pwa_ref, y_s = rest[-n_scr:-n_scr + 6]
    set_a = rest[-2 * N_CHUNK_BUF:-N_CHUNK_BUF]
    set_b = rest[-N_CHUNK_BUF:]
    prevs = (pr_ref, pk_ref, pv_ref, pwa_ref)
    s = pl.program_id(0)

    @pl.when(s == 0)
    def _():
        s_ref[...] = jnp.zeros_like(s_ref)
        for ref in prevs + tuple(set_b):
            ref[...] = jnp.zeros_like(ref)

    args = (r_ref, k_ref, v_ref, wa_ref, gr_ref, vecs_ref, w2a2_ref, tri_ref, pmat_ref, gmask_ref,
            y_ref, s_ref, prevs, y_s)

    @pl.when((s & 1) == 0)
    def _():
        _wkv_prompt_step(*args, set_a, set_b)

    @pl.when((s & 1) == 1)
    def _():
        _wkv_prompt_step(*args, set_b, set_a)

    @pl.when(s == pl.num_programs(0) - 1)
    def _():
        for b in range(BATCH):
            for g in range(N_HG):
                for j in range(HG):
                    s_out_ref[b, g * HG + j] = s_ref[b, g, j * HEAD:(j + 1) * HEAD, :][:, j * HEAD:(j + 1) * HEAD]
        sh_out_ref[:, :, 0:D_R] = pr_ref[...]
        sh_out_ref[:, :, D_R:2 * D_R] = pk_ref[...]
        sh_out_ref[:, :, 2 * D_R:3 * D_R] = pv_ref[...]
        sh_out_ref[:, :, 3 * D_R:D_SHIFT] = pwa_ref[...]


def _wkv_prompt_step(r_ref, k_ref, v_ref, wa_ref, gr_ref, vecs_ref, w2a2_ref, tri_ref, pmat_ref, gmask_ref,
                     y_ref, s_ref, prevs, y_s, wset, rset):
    L, nb = WKV_L, BATCH
    vecs = vecs_ref[...]
    pmat = pmat_ref[...]
    gmask = gmask_ref[...]
    gmask_f = gmask.astype(F32)
    tri = tri_ref[...]
    rs = lambda b: slice(b * L, (b + 1) * L)
    sl = lambda g: slice(g * GL, (g + 1) * GL)
    chains = [(b, g) for b in range(nb) for g in range(N_HG)]
    lhs_r, p_r, ak_r, ar_r, zd_r, v_r, bonus_r, wl_r = rset
    lhs_w, p_w, ak_w, ar_w, zd_w, v_w, bonus_w, wl_w = wset
    pr_ref, pk_ref, pv_ref, pwa_ref = prevs

    st = {}

    def b_state_terms(n, b, g):
        st["gs", n] = _mm_nt(lhs_r[n], s_ref[b, g])

    def b_rhs(n, b, g):
        st["vbd", n] = _block_diag(v_r[rs(b), sl(g)], gmask)
        st["rhs", n] = st["gs", n][:L] + _dot(ak_r[n], st["vbd", n])

    def b_u(n, b, g):
        st["u", n] = _dot(p_r[n], _block_diag(st["rhs", n], gmask))

    def b_y(n, b, g):
        y_s[rs(b), sl(g)] = st["gs", n][L:] + _dot(
            ar_r[n], jnp.concatenate([_block_diag(st["u", n], gmask), st["vbd", n]], axis=0))

    def b_state(n, b, g):
        uv = jnp.concatenate([st["u", n], v_r[rs(b), sl(g)]], axis=0)
        s_ref[b, g] = (s_ref[b, g] * wl_r[b, 0:1, sl(g)] + _mm_tn(uv, zd_r[n])) * gmask_f

    def staged(fns, members):
        return [(lambda f=f, n=n, b=b, g=g: f(n, b, g)) for f in fns for n, (b, g) in members]

    state_part = staged([b_state_terms, b_rhs, b_u, b_y, b_state], list(enumerate(chains)))

    row = _iota((L, GL), 0)
    col = _iota((L, GL), 1) & (HEAD - 1)
    strict = row > col
    incl2 = jnp.concatenate([row >= col] * 2, axis=1)
    eye = jnp.where(row == col, 1.0, 0.0)
    pp = {}

    def prep_pieces(b):
        def mix_r():
            x = r_ref[b].astype(F32)
            pp["xr", b] = _shift_mix(x, _prev_rows(x, pr_ref[b:b + 1], 1), vecs[V_MU_R:V_MU_R + 1, :])
            pr_ref[b] = x[L - 1:L, :]

        def mix_k():
            x = k_ref[b].astype(F32)
            pp["xk", b] = _shift_mix(x, _prev_rows(x, pk_ref[b:b + 1], 1), vecs[V_MU_K:V_MU_K + 1, :])
            pk_ref[b] = x[L - 1:L, :]

        def mix_v():
            x = v_ref[b].astype(F32)
            xv = _shift_mix(x, _prev_rows(x, pv_ref[b:b + 1], 1), vecs[V_MU_V:V_MU_V + 1, :])
            pp["xv", b] = xv
            v_w[rs(b), :] = xv
            pv_ref[b] = x[L - 1:L, :]

        def lowrank():
            x = wa_ref[b].astype(F32)
            xwa = _shift_mix(x, _prev_rows(x, pwa_ref[b:b + 1], 1), vecs[V_MU_WA:V_MU_WA + 1, :2 * R_W])
            pwa_ref[b] = x[L - 1:L, :]
            pp["lr", b] = _dot(_lowrank_in(xwa), w2a2_ref[...])

        def gates(g):
            def run():
                xr, xk = pp["xr", b][:, sl(g)], pp["xk", b][:, sl(g)]
                lr = pp["lr", b]
                gt = _prep_gates(xr, xk, lr[:, sl(g)], lr[:, D_R + g * GL:D_R + (g + 1) * GL], vecs[:, sl(g)])
                pp["gt", b, g] = gt
                pp["cl", b, g] = _cumsum_rows(tri, gt["lw"])
                pp["seg", b, g] = _segsum(gt["seg_in"], pmat)
            return run

        def scale(g):
            def run():
                n = b * N_HG + g
                ops = _prep_scale(pp["xr", b][:, sl(g)], pp["xv", b][:, sl(g)], pp["gt", b, g], pp["cl", b, g],
                                  pp["seg", b, g], lambda x: jnp.broadcast_to(x[L - 1:L, :], x.shape))
                bonus_w[rs(b), sl(g)] = ops["bonus"]
                wl_w[b, :, sl(g)] = ops["wl"][0:8, :]
                lhs = jnp.concatenate([ops["at"], ops["rt"]], axis=0).astype(BF16)
                lhs_w[n] = lhs
                zd_w[n] = jnp.concatenate([ops["bd"], ops["kd"]], axis=0).astype(BF16)
                pp["lhs", n] = lhs
                pp["bk", n] = jnp.concatenate([_block_diag(ops["bt"], gmask), _block_diag(ops["kt"], gmask)],
                                              axis=0)
            return run

        return ([mix_r, mix_k, mix_v, lowrank] + [gates(g) for g in range(N_HG)]
                + [scale(g) for g in range(N_HG)])

    def a_scores(n, b, g):
        sc = _mm_nt(pp["lhs", n], pp["bk", n])
        a_ab = jnp.where(strict, sc[:L, :GL], 0.0)
        ak_w[n] = jnp.where(strict, sc[:L, GL:], 0.0).astype(BF16)
        ar_w[n] = jnp.where(incl2, sc[L:], 0.0).astype(BF16)
        pp["x", n] = a_ab.astype(BF16)
        pp["xbd", n] = _block_diag(a_ab, gmask)
        pp["p", n] = eye + a_ab

    def a_square(n, b, g):
        x = _dot(pp["x", n], pp["xbd", n]).astype(BF16)
        pp["x", n], pp["xbd", n] = x, _block_diag(x, gmask)

    def a_double(n, b, g):
        xp = _dot(jnp.concatenate([pp["x", n], pp["p", n].astype(BF16)], axis=0), pp["xbd", n])
        x = xp[:L].astype(BF16)
        pp["x", n], pp["xbd", n] = x, _block_diag(x, gmask)
        pp["p", n] = pp["p", n] + xp[L:]

    def a_inverse(n, b, g):
        p_w[n] = (pp["p", n] + _dot(pp["p", n].astype(BF16), pp["xbd", n])).astype(BF16)

    inv_stages = [a_scores, a_square] + [a_double] * (int(math.log2(L)) - 2) + [a_inverse]
    half = N_CHAIN // 2
    members = list(enumerate(chains))

    fin = {}

    def fin_mean(g):
        def run():
            y = y_s[:, sl(g)]
            fin[g] = y - _segsum(y, pmat) * (1.0 / HEAD)
        return run

    def fin_out(g):
        def run():
            d = fin[g]
            var = _segsum(d * d, pmat) * (1.0 / HEAD)
            yn = d * lax.rsqrt(var + GN_EPS) * vecs[V_LNG:V_LNG + 1, sl(g)] + vecs[V_LNB:V_LNB + 1, sl(g)]
            g_r = gr_ref[:, :, sl(g)].astype(F32).reshape(nb * L, GL)
            out = (yn + bonus_r[:, sl(g)]) * (g_r * jax.nn.sigmoid(g_r))
            y_ref[:, :, sl(g)] = out.astype(BF16).reshape(nb, L, GL)
        return run

    _interleave(state_part, prep_pieces(0) + prep_pieces(1))
    _interleave(staged(inv_stages, members[:half]), prep_pieces(2) + prep_pieces(3))
    _interleave(staged(inv_stages, members[half:]),
                [fin_mean(g) for g in range(N_HG)] + [fin_out(g) for g in range(N_HG)])


def _stack_io(bufs, n_in, first_out):
    return ([pl.BlockSpec(memory_space=pl.ANY)] * len(bufs), list(bufs),
            {n_in + i: first_out + i for i in range(len(bufs))})


def _wkv_prompt(z3, vecs, w2a2, tri, pmat, gmask, stacked, layer):
    L = WKV_L
    n_chunks = SEQ // L
    cur = lambda s: jnp.minimum(s, n_chunks - 1)
    prv = lambda s: jnp.maximum(s - 1, 0)
    zspec = lambda cb: pl.BlockSpec((BATCH, L, D_R), lambda s: (0, cur(s), cb))
    full = lambda a: pl.BlockSpec(a.shape, lambda s: (0,) * a.ndim)
    row_scr = lambda w: pltpu.VMEM((BATCH, 1, w), F32)
    extra_specs, extra_ops, aliases = _stack_io(stacked, 10, 1)
    return pl.pallas_call(
        _wkv_prompt_kernel,
        grid=(n_chunks + 1,),
        in_specs=[
            zspec(CB_R), zspec(CB_K), zspec(CB_V),
            pl.BlockSpec((BATCH, L, 2 * R_W), lambda s: (0, cur(s), CB_WA)),
            pl.BlockSpec((BATCH, L, D_R), lambda s: (0, prv(s), CB_GR)),
            pl.BlockSpec((None, N_VEC, D_R), lambda s: (layer, 0, 0)),
            pl.BlockSpec((None, 2 * R_W, 2 * D_R), lambda s: (layer, 0, 0)),
            full(tri), full(pmat), full(gmask),
        ] + extra_specs,
        out_specs=[
            pl.BlockSpec((BATCH, L, D_R), lambda s: (0, prv(s), 0)),
            pl.BlockSpec((None, BATCH, H_R, HEAD, HEAD), lambda s: (layer, 0, 0, 0, 0)),
            pl.BlockSpec((None, BATCH, 1, D_SHIFT), lambda s: (layer, 0, 0, 0)),
        ],
        out_shape=[
            jax.ShapeDtypeStruct((BATCH, SEQ, D_R), BF16),
            jax.ShapeDtypeStruct((DEPTH, BATCH, H_R, HEAD, HEAD), F32),
            jax.ShapeDtypeStruct((DEPTH, BATCH, 1, D_SHIFT), F32),
        ],
        scratch_shapes=[
            pltpu.VMEM((BATCH, N_HG, GL, GL), F32),
            row_scr(D_R), row_scr(D_R), row_scr(D_R), row_scr(2 * R_W),
            pltpu.VMEM((BATCH * L, D_R), F32),
        ] + _chunk_buffers() + _chunk_buffers(),
        input_output_aliases=aliases,
        compiler_params=_params("arbitrary"),
        name="wkv_prompt",
    )(z3, z3, z3, z3, z3, vecs, w2a2, tri, pmat, gmask, *extra_ops)


def _wkv_sample_kernel(r_ref, k_ref, v_ref, gr_ref, wa_ref, sh_in_ref, s_in_ref,
                       vecs_ref, w2a2_ref, tri_ref, pmat_ref, gmask_ref,
                       y_ref, s_out_ref, sh_out_ref, seg_s, u_s, zd_s):
    nb, T = SAMPLE_NB, DEC_SEQ
    R = nb * T
    vecs = vecs_ref[...]
    pmat = pmat_ref[...]
    gmask = gmask_ref[...]
    bc = lambda x, j: _bcast_row(x, nb, j)

    r, k, v, wa = [ref[...].astype(F32) for ref in (r_ref, k_ref, v_ref, wa_ref)]
    cols = ((0, D_R), (D_R, 2 * D_R), (2 * D_R, 3 * D_R), (3 * D_R, D_SHIFT))
    prev = [_prev_rows(x, sh_in_ref[:, :, lo:hi], nb) for x, (lo, hi) in zip((r, k, v, wa), cols)]
    for x, (lo, hi) in zip((r, k, v, wa), cols):
        sh_out_ref[:, :, lo:hi] = _grouped(x, nb)[:, T - 1:T, :]
    ops = _wkv_prep(r, k, v, wa, *prev, vecs, w2a2_ref[...], tri_ref[...], pmat, lambda x: bc(x, T - 1))
    rt, kt, at, bt, vv = ops["rt"], ops["kt"], ops["at"], ops["bt"], ops["v"]
    zd_s[...] = jnp.concatenate([ops["bd"], ops["kd"]], axis=0)

    for j in range(T):
        btj, ktj = bc(bt, j), bc(kt, j)
        seg = _segsum(jnp.concatenate([at * btj, at * ktj, rt * btj, rt * ktj], axis=0), pmat)
        for q in range(4):
            seg_s[j, q] = seg[q * R:(q + 1) * R]

    ga_rows, gr_rows = [], []
    for b in range(nb):
        ga_g, gr_g = [], []
        for g in range(N_HG):
            sl = slice(g * GL, (g + 1) * GL)
            s4 = s_in_ref[b, g * HG:(g + 1) * HG].reshape(GL, HEAD)
            s_bd = jnp.concatenate([s4.astype(BF16)] * HG, axis=1) * gmask
            lhs = jnp.concatenate([at[b * T:(b + 1) * T, sl], rt[b * T:(b + 1) * T, sl]], axis=0)
            gs = _mm_nt(lhs, s_bd)
            ga_g.append(gs[:T])
            gr_g.append(gs[T:])
        ga_rows.append(jnp.concatenate(ga_g, axis=1))
        gr_rows.append(jnp.concatenate(gr_g, axis=1))
    g_a = jnp.concatenate(ga_rows, axis=0)
    g_r = jnp.concatenate(gr_rows, axis=0)

    t_idx = _iota((R, D_R), 0) & (T - 1)
    u = g_a
    for j in range(T - 1):
        u = u + jnp.where(t_idx > j, seg_s[j, 1] * bc(vv, j), 0.0)
    for j in range(T - 1):
        u = u + jnp.where(t_idx > j, seg_s[j, 0] * bc(u, j), 0.0)
    y = g_r
    for j in range(T):
        y = y + jnp.where(t_idx >= j, seg_s[j, 2] * bc(u, j) + seg_s[j, 3] * bc(vv, j), 0.0)
    y_ref[...] = _wkv_finish(y, ops["bonus"], gr_ref[...].astype(F32), vecs, pmat)

    u_s[...] = jnp.concatenate([u, vv], axis=0)
    wl = ops["wl"]
    row2 = _iota((2 * R, GL), 0) & (R - 1)
    for g in range(N_HG):
        sl = slice(g * GL, (g + 1) * GL)
        uv_g = u_s[:, sl]
        zd_g = zd_s[:, sl]
        for b in range(nb):
            mine = (row2 >= b * T) & (row2 < (b + 1) * T)
            upd = _mm_tn(uv_g, jnp.where(mine, zd_g, 0.0))
            for j in range(HG):
                h = g * HG + j
                s_out_ref[b, h] = (s_in_ref[b, h] * wl[b * T:b * T + 1, h * HEAD:(h + 1) * HEAD]
                                   + upd[j * HEAD:(j + 1) * HEAD, j * HEAD:(j + 1) * HEAD])


def _wkv_sample(z, shift_buf, state_buf, vecs, w2a2, tri, pmat, gmask, layer):
    nb, T = SAMPLE_NB, DEC_SEQ
    R = nb * T
    zspec = lambda cb: pl.BlockSpec((R, D_R), lambda i: (i, cb))
    full = lambda a: pl.BlockSpec(a.shape, lambda i: (0,) * a.ndim)
    sspec = pl.BlockSpec((None, nb, H_R, HEAD, HEAD), lambda i: (layer, i, 0, 0, 0))
    shspec = pl.BlockSpec((None, nb, 1, D_SHIFT), lambda i: (layer, i, 0, 0))
    return pl.pallas_call(
        _wkv_sample_kernel,
        grid=(DEC_BATCH // nb,),
        in_specs=[
            zspec(CB_R), zspec(CB_K), zspec(CB_V), zspec(CB_GR),
            pl.BlockSpec((R, 2 * R_W), lambda i: (i, CB_WA)),
            shspec, sspec,
            pl.BlockSpec((None, N_VEC, D_R), lambda i: (layer, 0, 0)),
            pl.BlockSpec((None, 2 * R_W, 2 * D_R), lambda i: (layer, 0, 0)),
            full(tri), full(pmat), full(gmask),
        ],
        out_specs=[pl.BlockSpec((R, D_R), lambda i: (i, 0)), sspec, shspec],
        out_shape=[
            jax.ShapeDtypeStruct((N_SAMPLE, D_R), BF16),
            jax.ShapeDtypeStruct((DEPTH, DEC_BATCH, H_R, HEAD, HEAD), F32),
            jax.ShapeDtypeStruct((DEPTH, DEC_BATCH, 1, D_SHIFT), F32),
        ],
        scratch_shapes=[
            pltpu.VMEM((T, 4, R, D_R), F32),
            pltpu.VMEM((2 * R, D_R), F32),
            pltpu.VMEM((2 * R, D_R), F32),
        ],
        input_output_aliases={5: 2, 6: 1},
        compiler_params=_params("arbitrary"),
        name="wkv_sample",
    )(z, z, z, z, z, shift_buf, state_buf, vecs, w2a2, tri, pmat, gmask)


def _layer_norm(x, g, b):
    mu = jnp.mean(x, axis=-1, keepdims=True)
    d = x - mu
    var = jnp.mean(d * d, axis=-1, keepdims=True)
    return d * lax.rsqrt(var + LN_EPS) * g + b


def _sgu_prompt_kernel(u_ref, vg_ref, gg_ref, lng_ref, lnb_ref, w_ref, bias_ref, y_ref):
    L = SGU_L
    vn = _layer_norm(vg_ref[...].astype(F32), lng_ref[...], lnb_ref[...]).astype(BF16)
    causal = _iota((L, L), 0) >= _iota((L, L), 1)
    wm = [jnp.where(causal, w_ref[g], 0.0).astype(BF16) for g in range(N_GROUPS)]
    rows = []
    for c in range(SGU_CHUNKS):
        parts = [_dot(wm[g], vn[c * L:(c + 1) * L, g * GW:(g + 1) * GW]) for g in range(N_GROUPS)]
        rows.append(jnp.concatenate(parts, axis=1) + bias_ref[...])
    s = jnp.concatenate(rows, axis=0)
    gg = gg_ref[...].astype(F32)
    y_ref[...] = (u_ref[...].astype(F32) * s * (gg * jax.nn.sigmoid(gg))).astype(BF16)


def _sgu_prompt(z, ln_g, ln_b, sgu_w, bias_exp, layer):
    L = SGU_L
    rows = L * SGU_CHUNKS
    zspec = lambda cb: pl.BlockSpec((rows, D_G), lambda i: (i, cb))
    vspec = pl.BlockSpec((None, 1, D_G), lambda i: (layer, 0, 0))
    return pl.pallas_call(
        _sgu_prompt_kernel,
        grid=(N_PROMPT // rows,),
        in_specs=[
            zspec(CB_U), zspec(CB_VG), zspec(CB_GG), vspec, vspec,
            pl.BlockSpec((None, N_GROUPS, L, L), lambda i: (layer, 0, 0, 0)),
            pl.BlockSpec((None, L, D_G), lambda i: (layer, 0, 0)),
        ],
        out_specs=pl.BlockSpec((rows, D_G), lambda i: (i, 0)),
        out_shape=jax.ShapeDtypeStruct((N_PROMPT, D_G), BF16),
        compiler_params=_params("arbitrary"),
        name="sgu_prompt",
    )(z, z, z, ln_g, ln_b, sgu_w, bias_exp)


def _sgu_sample_kernel(u_ref, vg_ref, gg_ref, lng_ref, lnb_ref, wc_ref, bias_ref, *rest):
    y_ref, vn_ref = rest[-2:]
    T = DEC_SEQ
    R = u_ref.shape[0]
    nb = R // T
    vn = _layer_norm(vg_ref[...].astype(F32), lng_ref[...], lnb_ref[...])
    vn_ref[...] = vn
    vn3 = vn.reshape(nb, T, D_G)
    s3 = jnp.broadcast_to(bias_ref[...][None], (nb, T, D_G))
    for j in range(T):
        s3 = s3 + jnp.broadcast_to(vn3[:, j:j + 1, :], vn3.shape) * wc_ref[j][None]
    gg = gg_ref[...].astype(F32)
    y_ref[...] = (u_ref[...].astype(F32) * s3.reshape(R, D_G) * (gg * jax.nn.sigmoid(gg))).astype(BF16)


def _sgu_sample(z, ln_g, ln_b, wc, bias_c, vn_buf, layer):
    R = 128
    zspec = lambda cb: pl.BlockSpec((R, D_G), lambda i: (i, cb))
    vspec = pl.BlockSpec((None, 1, D_G), lambda i: (layer, 0, 0))
    extra_specs, extra_ops, aliases = _stack_io((vn_buf,), 7, 1)
    return pl.pallas_call(
        _sgu_sample_kernel,
        grid=(N_SAMPLE // R,),
        in_specs=[
            zspec(CB_U), zspec(CB_VG), zspec(CB_GG), vspec, vspec,
            pl.BlockSpec((None, DEC_SEQ, DEC_SEQ, D_G), lambda i: (layer, 0, 0, 0)),
            pl.BlockSpec((None, DEC_SEQ, D_G), lambda i: (layer, 0, 0)),
        ] + extra_specs,
        out_specs=[pl.BlockSpec((R, D_G), lambda i: (i, 0)),
                   pl.BlockSpec((None, R, D_G), lambda i: (layer, i, 0))],
        out_shape=[jax.ShapeDtypeStruct((N_SAMPLE, D_G), BF16),
                   jax.ShapeDtypeStruct((DEPTH, N_SAMPLE, D_G), F32)],
        input_output_aliases=aliases,
        compiler_params=_params("arbitrary"),
        name="sgu_sample",
    )(z, z, z, ln_g, ln_b, wc, bias_c, *extra_ops)


def _outproj_kernel(ya_ref, yb_ref, ga_ref, gb_ref, h_ref, wa_ref, wb_ref, wo_ref, gn_ref, *outs):
    ya = _dot(ya_ref[...], wa_ref[...])
    yb = _dot(yb_ref[...], wb_ref[...])
    m = jax.nn.sigmoid(ga_ref[...].astype(F32)) * ya + jax.nn.sigmoid(gb_ref[...].astype(F32)) * yb
    h_new = h_ref[...] + _dot(m.astype(BF16), wo_ref[...])
    xn = _rms_scale(h_new, gn_ref[...])
    if len(outs) == 1:
        outs[0][...] = xn
    else:
        outs[0][...] = h_new
        outs[1][...] = xn.astype(BF16)


def _outproj(ya_in, yb_in, z, h, wpa, wpb, wout, gn, layer, last):
    n_tok = h.shape[0]
    tm = TM_OUT
    wspec = lambda a: pl.BlockSpec((None,) + a.shape[1:], lambda i: (layer, 0, 0),
                                   pipeline_mode=pl.Buffered(1))
    row = lambda w: pl.BlockSpec((tm, w), lambda i: (i, 0))
    tok = lambda dt: jax.ShapeDtypeStruct((n_tok, D_MODEL), dt)
    return pl.pallas_call(
        _outproj_kernel,
        grid=(n_tok // tm,),
        in_specs=[
            row(D_R), row(D_G), row(D_MODEL),
            pl.BlockSpec((tm, D_MODEL), lambda i: (i, 1)),
            row(D_MODEL),
            wspec(wpa), wspec(wpb), wspec(wout),
            pl.BlockSpec((None, 1, D_MODEL), lambda i: (gn[1], 0, 0)),
        ],
        out_specs=[row(D_MODEL)] if last else [row(D_MODEL), row(D_MODEL)],
        out_shape=[tok(F32)] if last else [tok(F32), tok(BF16)],
        compiler_params=_params("arbitrary"),
        name="outproj",
    )(ya_in, yb_in, z, z, h, wpa, wpb, wout, gn[0])


def _block_tri(n, blk):
    i = jnp.arange(n)
    return ((i[:, None] >= i[None, :]) & (i[:, None] // blk == i[None, :] // blk)).astype(BF16)


def kernel(x_prompt, x_sample, state_wkv, state_shift, norm_g, w_in, shift_mu, w0, w2, a0, a2, k_k, k_a, r_k,
           lnx_g, lnx_b, sgu_ln_g, sgu_ln_b, sgu_w, sgu_b, w_proj_a, w_proj_b, w_out, final_norm_g):
    c_wd, c_gr = 3 * D_R, D_SHIFT
    wpa, wpb, wout = w_proj_a.astype(BF16), w_proj_b.astype(BF16), w_out.astype(BF16)

    zrow = jnp.zeros((DEPTH, D_R), F32)
    mu_wa = jnp.pad(shift_mu[:, c_wd:c_gr], ((0, 0), (0, D_R - 2 * R_W)))
    vec_rows = [shift_mu[:, 0:D_R], shift_mu[:, D_R:2 * D_R], shift_mu[:, 2 * D_R:3 * D_R], w0, a0, k_k, k_a,
                r_k.reshape(DEPTH, D_R), lnx_g, lnx_b, mu_wa] + [zrow] * (N_VEC - 11)
    vecs = jnp.stack(vec_rows, axis=1)
    zblk = jnp.zeros((DEPTH, R_W, D_R), F32)
    w2a2 = jnp.concatenate([jnp.concatenate([w2, zblk], axis=2),
                            jnp.concatenate([zblk, a2], axis=2)], axis=1).astype(BF16)
    norm_g3 = norm_g.reshape(DEPTH, 1, D_MODEL)
    final_g3 = final_norm_g.reshape(1, 1, D_MODEL)
    ln_g3, ln_b3 = sgu_ln_g.reshape(DEPTH, 1, D_G), sgu_ln_b.reshape(DEPTH, 1, D_G)
    bias_exp = jnp.repeat(jnp.swapaxes(sgu_b, 1, 2), GW, axis=2)
    tmask = jnp.tril(jnp.ones((DEC_SEQ, DEC_SEQ), F32))
    wc = jnp.repeat(jnp.transpose(sgu_w[:, :, :DEC_SEQ, :DEC_SEQ] * tmask, (0, 3, 2, 1)), GW, axis=3)
    bias_c = bias_exp[:, :DEC_SEQ, :]

    pmat = ((jnp.arange(GL)[:, None] // HEAD) == (jnp.arange(GL)[None, :] // HEAD)).astype(BF16)
    tri_p = _block_tri(WKV_L, WKV_L)
    tri_s = _block_tri(SAMPLE_NB * DEC_SEQ, DEC_SEQ)
    src_cols = jnp.asarray(Z_SRC_UNITS, jnp.int32)

    h_p = x_prompt.reshape(N_PROMPT, D_MODEL)
    h_s = x_sample.reshape(N_SAMPLE, D_MODEL)
    xn_p = _norm_bf16(h_p, norm_g3, 0)
    xn_s = _norm_bf16(h_s, norm_g3, 0)
    stk_p = [jnp.zeros((DEPTH, BATCH, H_R, HEAD, HEAD), F32), jnp.zeros((DEPTH, BATCH, 1, D_SHIFT), F32)]
    wkv_s, shift_s = state_wkv, state_shift
    vn_s = jnp.zeros((DEPTH, N_SAMPLE, D_G), F32)
    for l in range(DEPTH):
        last = l == DEPTH - 1
        z_p = _inproj(xn_p, w_in, src_cols, l, TM_IN_PROMPT)
        z_s = _inproj(xn_s, w_in, src_cols, l, TM_IN_SAMPLE)
        ya_p, *stk_p = _wkv_prompt(z_p.reshape(BATCH, SEQ, NZ), vecs, w2a2, tri_p, pmat, pmat, stk_p, l)
        ya_s, wkv_s, shift_s = _wkv_sample(z_s, shift_s, wkv_s, vecs, w2a2, tri_s, pmat, pmat, l)
        yb_p = _sgu_prompt(z_p, ln_g3, ln_b3, sgu_w, bias_exp, l)
        yb_s, vn_s = _sgu_sample(z_s, ln_g3, ln_b3, wc, bias_c, vn_s, l)
        gn = (final_g3, 0) if last else (norm_g3, l + 1)
        out_p = _outproj(ya_p.reshape(N_PROMPT, D_R), yb_p, z_p, h_p, wpa, wpb, wout, gn, l, last)
        out_s = _outproj(ya_s, yb_s, z_s, h_s, wpa, wpb, wout, gn, l, last)
        if last:
            y_p, y_s = out_p[0], out_s[0]
        else:
            (h_p, xn_p), (h_s, xn_s) = out_p, out_s
    return (y_p.reshape(BATCH, SEQ, D_MODEL), y_s.reshape(DEC_BATCH, DEC_SEQ, D_MODEL),
            stk_p[0], stk_p[1], wkv_s, shift_s, vn_s.reshape(DEPTH, DEC_BATCH, DEC_SEQ, D_G))
```

```python
import math

import jax
import jax.numpy as jnp
from jax import lax
from jax.experimental import pallas as pl
from jax.experimental.pallas import tpu as pltpu

F32 = jnp.float32
BF16 = jnp.bfloat16

D_MODEL = 2048
BATCH = 4
SEQ = 2048
DEPTH = 4
DEC_BATCH = 128
DEC_SEQ = 8
HEAD = 64
D_R = 1024
H_R = 16
R_W = 64
R_A = 64
D_G = 1024
N_GROUPS = 8
GW = 128
D_SHIFT = 3 * D_R + R_W + R_A
RMS_EPS = 1e-6
LN_EPS = 1e-5
GN_EPS = 64e-5
EXP_M05 = math.exp(-0.5)

N_PROMPT = BATCH * SEQ
N_SAMPLE = DEC_BATCH * DEC_SEQ

LANE = 128
CB_U, CB_VG, CB_GG, CB_R, CB_K, CB_V, CB_GR = 4, 5, 6, 7, 8, 9, 10
CB_WA = 88
TN_IN = 512
NZ = 23 * TN_IN
Z_SRC_UNITS = tuple([57 + 4 * j for j in range(8)] + [33 + 4 * j for j in range(6)] + [4 * j for j in range(6)]
                    + [25, 29] + [24])

HG = 4
GL = HG * HEAD
N_HG = H_R // HG

WKV_L = 64
N_CHAIN = BATCH * N_HG
SAMPLE_NB = 8

TM_IN_PROMPT, TM_IN_SAMPLE = 4096, 1024
TM_IN_NORM = 2048
TM_OUT = 512
SGU_L = 128
SGU_CHUNKS = 4
VMEM_LIMIT = 56 * 1024 * 1024

V_MU_R, V_MU_K, V_MU_V, V_W0, V_A0, V_KK, V_KA, V_RK, V_LNG, V_LNB, V_MU_WA = range(11)
N_VEC = 16


def _dot(a, b):
    return jnp.dot(a, b, preferred_element_type=F32)


def _mm(a, b):
    return _dot(a.astype(BF16), b.astype(BF16))


def _mm_nt(a, b):
    return lax.dot_general(a.astype(BF16), b.astype(BF16), (((1,), (1,)), ((), ())),
                           preferred_element_type=F32)


def _mm_tn(a, b):
    return lax.dot_general(a.astype(BF16), b.astype(BF16), (((0,), (0,)), ((), ())),
                           preferred_element_type=F32)


def _iota(shape, dim):
    return lax.broadcasted_iota(jnp.int32, shape, dim)


def _block_diag(x, gmask_bf):
    reps = GL // x.shape[0]
    return jnp.concatenate([x.astype(BF16)] * reps, axis=0) * gmask_bf


def _segsum(x, pmat):
    rows, n = x.shape[0], x.shape[1] // GL
    xb = x.astype(BF16)
    stacked = jnp.concatenate([xb[:, g * GL:(g + 1) * GL] for g in range(n)], axis=0)
    s = _dot(stacked, pmat)
    return jnp.concatenate([s[g * rows:(g + 1) * rows] for g in range(n)], axis=1)


def _cumsum_rows(tri, x):
    hi = x.astype(BF16)
    lo = (x - hi.astype(F32)).astype(BF16)
    return _dot(tri, hi) + _dot(tri, lo)


def _rms_scale(x, g):
    ms = jnp.mean(x * x, axis=-1, keepdims=True)
    return x * lax.rsqrt(ms + RMS_EPS) * g


def _params(*sem):
    return pltpu.CompilerParams(dimension_semantics=sem, vmem_limit_bytes=VMEM_LIMIT)


def _inproj_kernel(cols_ref, x_ref, w_ref, z_ref):
    del cols_ref
    z_ref[...] = _dot(x_ref[...], w_ref[...].astype(BF16)).astype(z_ref.dtype)


def _inproj_norm_kernel(cols_ref, x_ref, g_ref, w_ref, z_ref, xn_ref):
    del cols_ref

    @pl.when(pl.program_id(1) == 0)
    def _():
        xn_ref[...] = _rms_scale(x_ref[...], g_ref[...]).astype(BF16)

    z_ref[...] = _dot(xn_ref[...], w_ref[...].astype(BF16)).astype(z_ref.dtype)


def _inproj(x, w_in, src_cols, layer, tm, norm_g3=None):
    n_tok = x.shape[0]
    wspec = pl.BlockSpec((None, pl.Element(D_MODEL), pl.Element(TN_IN)),
                         lambda i, j, units: (layer, 0, units[j] * LANE))
    if norm_g3 is None:
        kern, scratch, operands = _inproj_kernel, [], (x, w_in)
        in_specs = [pl.BlockSpec((tm, D_MODEL), lambda i, j, units: (i, 0)), wspec]
    else:
        kern, scratch, operands = _inproj_norm_kernel, [pltpu.VMEM((tm, D_MODEL), BF16)], (x, norm_g3, w_in)
        in_specs = [pl.BlockSpec((tm, D_MODEL), lambda i, j, units: (i, 0), pipeline_mode=pl.Buffered(1)),
                    pl.BlockSpec((None, 1, D_MODEL), lambda i, j, units: (layer, 0, 0)), wspec]
    grid_spec = pltpu.PrefetchScalarGridSpec(
        num_scalar_prefetch=1,
        grid=(n_tok // tm, NZ // TN_IN),
        in_specs=in_specs,
        out_specs=pl.BlockSpec((tm, TN_IN), lambda i, j, units: (i, j)),
        scratch_shapes=scratch,
    )
    return pl.pallas_call(
        kern,
        grid_spec=grid_spec,
        out_shape=jax.ShapeDtypeStruct((n_tok, NZ), BF16),
        compiler_params=_params("arbitrary", "arbitrary"),
        name="inproj",
    )(src_cols, *operands)


def _shift_mix(x, x_prev, mu):
    return x + (x_prev - x) * mu


def _lowrank_in(xwa):
    return jnp.where(_iota(xwa.shape, 1) < R_W, jnp.tanh(xwa), xwa).astype(BF16)


def _prep_gates(xr, xk, lr_w, lr_a, vecs):
    vec = lambda i: vecs[i:i + 1, :]
    lw = -EXP_M05 * jax.nn.sigmoid(vec(V_W0) + lr_w)
    ag = jax.nn.sigmoid(vec(V_A0) + lr_a)
    kk = xk * vec(V_KK)
    k2 = xk * (1.0 + (ag - 1.0) * vec(V_KA))
    return dict(lw=lw, ag=ag, kk=kk, k2=k2, seg_in=jnp.concatenate([kk * kk, xr * k2 * vec(V_RK)], axis=0))


def _prep_scale(xr, xv, gt, cl, seg, last_row):
    rows = cl.shape[0]
    kkn = gt["kk"] / jnp.maximum(jnp.sqrt(seg[:rows]), 1e-12)
    cl_last = last_row(cl)
    e_neg = jnp.exp(-cl)
    e_rem = jnp.exp(cl_last - cl)
    bb = kkn * gt["ag"]
    return dict(
        rt=xr * jnp.exp(cl),
        kt=gt["k2"] * e_neg,
        at=-kkn * jnp.exp(cl - gt["lw"]),
        bt=bb * e_neg,
        kd=gt["k2"] * e_rem,
        bd=bb * e_rem,
        v=xv,
        wl=jnp.exp(cl_last),
        bonus=seg[rows:] * xv,
    )


def _wkv_prep(r, k, v, wa, r_prev, k_prev, v_prev, wa_prev, vecs, w2a2, tri, pmat, last_row):
    vec = lambda i: vecs[i:i + 1, :]
    xr = _shift_mix(r, r_prev, vec(V_MU_R))
    xk = _shift_mix(k, k_prev, vec(V_MU_K))
    xv = _shift_mix(v, v_prev, vec(V_MU_V))
    lowrank = _dot(_lowrank_in(_shift_mix(wa, wa_prev, vecs[V_MU_WA:V_MU_WA + 1, :2 * R_W])), w2a2)
    gt = _prep_gates(xr, xk, lowrank[:, :D_R], lowrank[:, D_R:], vecs)
    return _prep_scale(xr, xv, gt, _cumsum_rows(tri, gt["lw"]), _segsum(gt["seg_in"], pmat), last_row)


def _wkv_finish(y, bonus, g_r, vecs, pmat):
    mu = _segsum(y, pmat) * (1.0 / HEAD)
    d = y - mu
    var = _segsum(d * d, pmat) * (1.0 / HEAD)
    yn = d * lax.rsqrt(var + GN_EPS) * vecs[V_LNG:V_LNG + 1, :] + vecs[V_LNB:V_LNB + 1, :]
    return ((yn + bonus) * (g_r * jax.nn.sigmoid(g_r))).astype(BF16)


def _grouped(x, nb):
    return x.reshape(nb, x.shape[0] // nb, x.shape[-1])


def _bcast_row(x, nb, j):
    x3 = _grouped(x, nb)
    return jnp.broadcast_to(x3[:, j:j + 1, :], x3.shape).reshape(x.shape)


def _prev_rows(x, prev, nb):
    t = x.shape[0] // nb
    first = (_iota(x.shape, 0) & (t - 1)) == 0
    return jnp.where(first, jnp.broadcast_to(prev, (nb, t, x.shape[-1])).reshape(x.shape), pltpu.roll(x, 1, 0))


def _interleave(major, minor):
    done = 0
    for i, piece in enumerate(major):
        piece()
        upto = (i + 1) * len(minor) // len(major)
        for other in minor[done:upto]:
            other()
        done = upto
    for other in minor[done:]:
        other()


def _chunk_buffers():
    L = WKV_L
    return [
        pltpu.VMEM((N_CHAIN, 2 * L, GL), BF16),
        pltpu.VMEM((N_CHAIN, L, GL), BF16),
        pltpu.VMEM((N_CHAIN, 2 * L, GL), F32),
        pltpu.VMEM((N_CHAIN, L, GL), BF16),
        pltpu.VMEM((N_CHAIN, 2 * L, GL), BF16),
        pltpu.VMEM((BATCH * L, D_R), F32),
        pltpu.VMEM((BATCH * L, D_R), F32),
        pltpu.VMEM((BATCH, 8, D_R), F32),
    ]


N_CHUNK_BUF = 8


def _wkv_prompt_kernel(r_ref, k_ref, v_ref, wa_ref, gr_ref, vecs_ref, w2a2_ref, tri_ref, pmat_ref, gmask_ref,
                       *rest):
    n_scr = 6 + 2 * N_CHUNK_BUF
    y_ref, s_out_ref, sh_out_ref = rest[-n_scr - 3:-n_scr]
    s_ref, pr_ref, pk_ref, pv_ref, pwa_ref, y_s = rest[-n_scr:-n_scr + 6]
    set_a = rest[-2 * N_CHUNK_BUF:-N_CHUNK_BUF]
    set_b = rest[-N_CHUNK_BUF:]
    prevs = (pr_ref, pk_ref, pv_ref, pwa_ref)
    s = pl.program_id(0)

    @pl.when(s == 0)
    def _():
        s_ref[...] = jnp.zeros_like(s_ref)
        for ref in prevs + tuple(set_b):
            ref[...] = jnp.zeros_like(ref)

    args = (r_ref, k_ref, v_ref, wa_ref, gr_ref, vecs_ref, w2a2_ref, tri_ref, pmat_ref, gmask_ref,
            y_ref, s_ref, prevs, y_s)

    @pl.when((s & 1) == 0)
    def _():
        _wkv_prompt_step(*args, set_a, set_b)

    @pl.when((s & 1) == 1)
    def _():
        _wkv_prompt_step(*args, set_b, set_a)

    @pl.when(s == pl.num_programs(0) - 1)
    def _():
        for b in range(BATCH):
            for g in range(N_HG):
                for j in range(HG):
                    s_out_ref[b, g * HG + j] = s_ref[b, g, j * HEAD:(j + 1) * HEAD, :][:, j * HEAD:(j + 1) * HEAD]
        sh_out_ref[:, :, 0:D_R] = pr_ref[...]
        sh_out_ref[:, :, D_R:2 * D_R] = pk_ref[...]
        sh_out_ref[:, :, 2 * D_R:3 * D_R] = pv_ref[...]
        sh_out_ref[:, :, 3 * D_R:D_SHIFT] = pwa_ref[...]


def _wkv_prompt_step(r_ref, k_ref, v_ref, wa_ref, gr_ref, vecs_ref, w2a2_ref, tri_ref, pmat_ref, gmask_ref,
                     y_ref, s_ref, prevs, y_s, wset, rset):
    L, nb = WKV_L, BATCH
    vecs = vecs_ref[...]
    pmat = pmat_ref[...]
    gmask = gmask_ref[...]
    gmask_f = gmask.astype(F32)
    tri = tri_ref[...]
    rs = lambda b: slice(b * L, (b + 1) * L)
    sl = lambda g: slice(g * GL, (g + 1) * GL)
    chains = [(b, g) for b in range(nb) for g in range(N_HG)]
    lhs_r, p_r, av_r, arb_r, zd_r, v_r, bonus_r, wl_r = rset
    lhs_w, p_w, av_w, arb_w, zd_w, v_w, bonus_w, wl_w = wset
    pr_ref, pk_ref, pv_ref, pwa_ref = prevs

    st = {}

    def b_state_terms(n, b, g):
        st["gs", n] = _mm_nt(lhs_r[n], s_ref[b, g])

    def b_u(n, b, g):
        st["u", n] = _dot(p_r[n], _block_diag(st["gs", n][:L] + av_r[n, 0:L, :], gmask))

    def b_y(n, b, g):
        y_s[rs(b), sl(g)] = st["gs", n][L:] + av_r[n, L:2 * L, :] + _dot(arb_r[n], _block_diag(st["u", n], gmask))

    def b_state(n, b, g):
        uv = jnp.concatenate([st["u", n], v_r[rs(b), sl(g)]], axis=0)
        s_ref[b, g] = (s_ref[b, g] * wl_r[b, 0:1, sl(g)] + _mm_tn(uv, zd_r[n])) * gmask_f

    def staged(fns, members):
        return [(lambda f=f, n=n, b=b, g=g: f(n, b, g)) for f in fns for n, (b, g) in members]

    state_part = staged([b_state_terms, b_u, b_y, b_state], list(enumerate(chains)))

    row = _iota((L, GL), 0)
    col = _iota((L, GL), 1) & (HEAD - 1)
    strict = row > col
    incl = row >= col
    row2 = _iota((2 * L, GL), 0)
    col2 = _iota((2 * L, GL), 1) & (HEAD - 1)
    both = (row2 - jnp.where(row2 < L, 0, L - 1)) > col2
    eye = jnp.where(row == col, 1.0, 0.0)
    pp = {}

    def prep_pieces(b):
        def mix_r():
            x = r_ref[b].astype(F32)
            pp["xr", b] = _shift_mix(x, _prev_rows(x, pr_ref[b:b + 1], 1), vecs[V_MU_R:V_MU_R + 1, :])
            pr_ref[b] = x[L - 1:L, :]

        def mix_k():
            x = k_ref[b].astype(F32)
            pp["xk", b] = _shift_mix(x, _prev_rows(x, pk_ref[b:b + 1], 1), vecs[V_MU_K:V_MU_K + 1, :])
            pk_ref[b] = x[L - 1:L, :]

        def mix_v():
            x = v_ref[b].astype(F32)
            xv = _shift_mix(x, _prev_rows(x, pv_ref[b:b + 1], 1), vecs[V_MU_V:V_MU_V + 1, :])
            pp["xv", b] = xv
            v_w[rs(b), :] = xv
            pv_ref[b] = x[L - 1:L, :]

        def lowrank():
            x = wa_ref[b].astype(F32)
            xwa = _shift_mix(x, _prev_rows(x, pwa_ref[b:b + 1], 1), vecs[V_MU_WA:V_MU_WA + 1, :2 * R_W])
            pwa_ref[b] = x[L - 1:L, :]
            pp["lr", b] = _dot(_lowrank_in(xwa), w2a2_ref[...])

        def gates(g):
            def run():
                xr, xk = pp["xr", b][:, sl(g)], pp["xk", b][:, sl(g)]
                lr = pp["lr", b]
                gt = _prep_gates(xr, xk, lr[:, sl(g)], lr[:, D_R + g * GL:D_R + (g + 1) * GL], vecs[:, sl(g)])
                pp["gt", b, g] = gt
                pp["cl", b, g] = _cumsum_rows(tri, gt["lw"])
                pp["seg", b, g] = _segsum(gt["seg_in"], pmat)
            return run

        def scale(g):
            def run():
                n = b * N_HG + g
                ops = _prep_scale(pp["xr", b][:, sl(g)], pp["xv", b][:, sl(g)], pp["gt", b, g], pp["cl", b, g],
                                  pp["seg", b, g], lambda x: jnp.broadcast_to(x[L - 1:L, :], x.shape))
                bonus_w[rs(b), sl(g)] = ops["bonus"]
                wl_w[b, :, sl(g)] = ops["wl"][0:8, :]
                lhs = jnp.concatenate([ops["at"], ops["rt"]], axis=0).astype(BF16)
                lhs_w[n] = lhs
                zd_w[n] = jnp.concatenate([ops["bd"], ops["kd"]], axis=0).astype(BF16)
                pp["lhs", n] = lhs
                pp["bk", n] = jnp.concatenate([_block_diag(ops["bt"], gmask), _block_diag(ops["kt"], gmask)],
                                              axis=0)
            return run

        return ([mix_r, mix_k, mix_v, lowrank] + [gates(g) for g in range(N_HG)]
                + [scale(g) for g in range(N_HG)])

    def a_scores(n, b, g):
        sc = _mm_nt(pp["lhs", n], pp["bk", n])
        a_ab = jnp.where(strict, sc[:L, :GL], 0.0)
        arb_w[n] = jnp.where(incl, sc[L:, :GL], 0.0).astype(BF16)
        pp["akv", n] = jnp.where(both, sc[:, GL:], 0.0).astype(BF16)
        pp["x", n] = a_ab.astype(BF16)
        pp["xbd", n] = _block_diag(a_ab, gmask)
        pp["p", n] = eye + a_ab

    def a_values(n, b, g):
        av_w[n] = _dot(pp["akv", n], _block_diag(pp["xv", b][:, sl(g)], gmask))

    def a_square(n, b, g):
        x = _dot(pp["x", n], pp["xbd", n]).astype(BF16)
        pp["x", n], pp["xbd", n] = x, _block_diag(x, gmask)

    def a_double(n, b, g):
        xp = _dot(jnp.concatenate([pp["x", n], pp["p", n].astype(BF16)], axis=0), pp["xbd", n])
        x = xp[:L].astype(BF16)
        pp["x", n], pp["xbd", n] = x, _block_diag(x, gmask)
        pp["p", n] = pp["p", n] + xp[L:]

    def a_inverse(n, b, g):
        p_w[n] = (pp["p", n] + _dot(pp["p", n].astype(BF16), pp["xbd", n])).astype(BF16)

    inv_stages = [a_scores, a_values, a_square] + [a_double] * (int(math.log2(L)) - 2) + [a_inverse]
    half = N_CHAIN // 2
    members = list(enumerate(chains))

    fin = {}

    def fin_mean(g):
        def run():
            y = y_s[:, sl(g)]
            fin[g] = y - _segsum(y, pmat) * (1.0 / HEAD)
        return run

    def fin_out(g):
        def run():
            d = fin[g]
            var = _segsum(d * d, pmat) * (1.0 / HEAD)
            yn = d * lax.rsqrt(var + GN_EPS) * vecs[V_LNG:V_LNG + 1, sl(g)] + vecs[V_LNB:V_LNB + 1, sl(g)]
            g_r = gr_ref[:, :, sl(g)].astype(F32).reshape(nb * L, GL)
            out = (yn + bonus_r[:, sl(g)]) * (g_r * jax.nn.sigmoid(g_r))
            y_ref[:, :, sl(g)] = out.astype(BF16).reshape(nb, L, GL)
        return run

    _interleave(state_part, prep_pieces(0) + prep_pieces(1))
    _interleave(staged(inv_stages, members[:half]), prep_pieces(2) + prep_pieces(3))
    _interleave(staged(inv_stages, members[half:]),
                [fin_mean(g) for g in range(N_HG)] + [fin_out(g) for g in range(N_HG)])


def _stack_io(bufs, n_in, first_out):
    return ([pl.BlockSpec(memory_space=pl.ANY)] * len(bufs), list(bufs),
            {n_in + i: first_out + i for i in range(len(bufs))})


def _wkv_prompt(z3, vecs, w2a2, tri, pmat, gmask, stacked, layer):
    L = WKV_L
    n_chunks = SEQ // L
    cur = lambda s: jnp.minimum(s, n_chunks - 1)
    prv = lambda s: jnp.maximum(s - 1, 0)
    zspec = lambda cb: pl.BlockSpec((BATCH, L, D_R), lambda s: (0, cur(s), cb))
    full = lambda a: pl.BlockSpec(a.shape, lambda s: (0,) * a.ndim)
    row_scr = lambda w: pltpu.VMEM((BATCH, 1, w), F32)
    extra_specs, extra_ops, aliases = _stack_io(stacked, 10, 1)
    return pl.pallas_call(
        _wkv_prompt_kernel,
        grid=(n_chunks + 1,),
        in_specs=[
            zspec(CB_R), zspec(CB_K), zspec(CB_V),
            pl.BlockSpec((BATCH, L, 2 * R_W), lambda s: (0, cur(s), CB_WA)),
            pl.BlockSpec((BATCH, L, D_R), lambda s: (0, prv(s), CB_GR)),
            pl.BlockSpec((None, N_VEC, D_R), lambda s: (layer, 0, 0)),
            pl.BlockSpec((None, 2 * R_W, 2 * D_R), lambda s: (layer, 0, 0)),
            full(tri), full(pmat), full(gmask),
        ] + extra_specs,
        out_specs=[
            pl.BlockSpec((BATCH, L, D_R), lambda s: (0, prv(s), 0)),
            pl.BlockSpec((None, BATCH, H_R, HEAD, HEAD), lambda s: (layer, 0, 0, 0, 0)),
            pl.BlockSpec((None, BATCH, 1, D_SHIFT), lambda s: (layer, 0, 0, 0)),
        ],
        out_shape=[
            jax.ShapeDtypeStruct((BATCH, SEQ, D_R), BF16),
            jax.ShapeDtypeStruct((DEPTH, BATCH, H_R, HEAD, HEAD), F32),
            jax.ShapeDtypeStruct((DEPTH, BATCH, 1, D_SHIFT), F32),
        ],
        scratch_shapes=[
            pltpu.VMEM((BATCH, N_HG, GL, GL), F32),
            row_scr(D_R), row_scr(D_R), row_scr(D_R), row_scr(2 * R_W),
            pltpu.VMEM((BATCH * L, D_R), F32),
        ] + _chunk_buffers() + _chunk_buffers(),
        input_output_aliases=aliases,
        compiler_params=_params("arbitrary"),
        name="wkv_prompt",
    )(z3, z3, z3, z3, z3, vecs, w2a2, tri, pmat, gmask, *extra_ops)


def _wkv_sample_kernel(r_ref, k_ref, v_ref, gr_ref, wa_ref, sh_in_ref, s_in_ref,
                       vecs_ref, w2a2_ref, tri_ref, pmat_ref, gmask_ref,
                       y_ref, s_out_ref, sh_out_ref, seg_s, u_s, zd_s, ga_s, gr_s):
    nb, T = SAMPLE_NB, DEC_SEQ
    R = nb * T
    vecs = vecs_ref[...]
    pmat = pmat_ref[...]
    gmask = gmask_ref[...]
    bc = lambda x, j: _bcast_row(x, nb, j)

    r, k, v, wa = [ref[...].astype(F32) for ref in (r_ref, k_ref, v_ref, wa_ref)]
    cols = ((0, D_R), (D_R, 2 * D_R), (2 * D_R, 3 * D_R), (3 * D_R, D_SHIFT))
    prev = [_prev_rows(x, sh_in_ref[:, :, lo:hi], nb) for x, (lo, hi) in zip((r, k, v, wa), cols)]
    for x, (lo, hi) in zip((r, k, v, wa), cols):
        sh_out_ref[:, :, lo:hi] = _grouped(x, nb)[:, T - 1:T, :]
    ops = _wkv_prep(r, k, v, wa, *prev, vecs, w2a2_ref[...], tri_ref[...], pmat, lambda x: bc(x, T - 1))
    rt, kt, at, bt, vv = ops["rt"], ops["kt"], ops["at"], ops["bt"], ops["v"]
    zd_s[...] = jnp.concatenate([ops["bd"], ops["kd"]], axis=0)

    sl = lambda g: slice(g * GL, (g + 1) * GL)
    groups = [(b, g) for b in range(nb) for g in range(N_HG)]

    def score_piece(j):
        def run():
            btj, ktj = bc(bt, j), bc(kt, j)
            seg = _segsum(jnp.concatenate([at * btj, at * ktj, rt * btj, rt * ktj], axis=0), pmat)
            for q in range(4):
                seg_s[j, q] = seg[q * R:(q + 1) * R]
        return run

    def state_term_piece(b, g):
        def run():
            s4 = s_in_ref[b, g * HG:(g + 1) * HG].reshape(GL, HEAD)
            s_bd = jnp.concatenate([s4.astype(BF16)] * HG, axis=1) * gmask
            lhs = jnp.concatenate([at[b * T:(b + 1) * T, sl(g)], rt[b * T:(b + 1) * T, sl(g)]], axis=0)
            gs = _mm_nt(lhs, s_bd)
            ga_s[b * T:(b + 1) * T, sl(g)] = gs[:T]
            gr_s[b * T:(b + 1) * T, sl(g)] = gs[T:]
        return run

    _interleave([state_term_piece(b, g) for b, g in groups], [score_piece(j) for j in range(T)])

    t_idx = _iota((R, D_R), 0) & (T - 1)
    u = ga_s[...]
    for j in range(T - 1):
        u = u + jnp.where(t_idx > j, seg_s[j, 1] * bc(vv, j), 0.0)
    for j in range(T - 1):
        u = u + jnp.where(t_idx > j, seg_s[j, 0] * bc(u, j), 0.0)
    u_s[...] = jnp.concatenate([u, vv], axis=0)
    wl = ops["wl"]
    acc = {"y": gr_s[...]}

    def y_piece(j):
        def run():
            acc["y"] = acc["y"] + jnp.where(t_idx >= j, seg_s[j, 2] * bc(u, j) + seg_s[j, 3] * bc(vv, j), 0.0)
        return run

    def finish_piece():
        y_ref[...] = _wkv_finish(acc["y"], ops["bonus"], gr_ref[...].astype(F32), vecs, pmat)

    row2 = _iota((2 * R, GL), 0) & (R - 1)
    uv_t = {}

    def state_piece(g, b):
        def run():
            if b == 0:
                uv_t[g] = u_s[:, sl(g)].T.astype(BF16)
            mine = (row2 >= b * T) & (row2 < (b + 1) * T)
            upd = _dot(uv_t[g], jnp.where(mine, zd_s[:, sl(g)], 0.0).astype(BF16))
            for j in range(HG):
                h = g * HG + j
                s_out_ref[b, h] = (s_in_ref[b, h] * wl[b * T:b * T + 1, h * HEAD:(h + 1) * HEAD]
                                   + upd[j * HEAD:(j + 1) * HEAD, j * HEAD:(j + 1) * HEAD])
        return run

    _interleave([state_piece(g, b) for g in range(N_HG) for b in range(nb)],
                [y_piece(j) for j in range(T)] + [finish_piece])


def _wkv_sample(z, shift_buf, state_buf, vecs, w2a2, tri, pmat, gmask, layer):
    nb, T = SAMPLE_NB, DEC_SEQ
    R = nb * T
    zspec = lambda cb: pl.BlockSpec((R, D_R), lambda i: (i, cb))
    full = lambda a: pl.BlockSpec(a.shape, lambda i: (0,) * a.ndim)
    sspec = pl.BlockSpec((None, nb, H_R, HEAD, HEAD), lambda i: (layer, i, 0, 0, 0))
    shspec = pl.BlockSpec((None, nb, 1, D_SHIFT), lambda i: (layer, i, 0, 0))
    return pl.pallas_call(
        _wkv_sample_kernel,
        grid=(DEC_BATCH // nb,),
        in_specs=[
            zspec(CB_R), zspec(CB_K), zspec(CB_V), zspec(CB_GR),
            pl.BlockSpec((R, 2 * R_W), lambda i: (i, CB_WA)),
            shspec, sspec,
            pl.BlockSpec((None, N_VEC, D_R), lambda i: (layer, 0, 0)),
            pl.BlockSpec((None, 2 * R_W, 2 * D_R), lambda i: (layer, 0, 0)),
            full(tri), full(pmat), full(gmask),
        ],
        out_specs=[pl.BlockSpec((R, D_R), lambda i: (i, 0)), sspec, shspec],
        out_shape=[
            jax.ShapeDtypeStruct((N_SAMPLE, D_R), BF16),
            jax.ShapeDtypeStruct((DEPTH, DEC_BATCH, H_R, HEAD, HEAD), F32),
            jax.ShapeDtypeStruct((DEPTH, DEC_BATCH, 1, D_SHIFT), F32),
        ],
        scratch_shapes=[
            pltpu.VMEM((T, 4, R, D_R), F32),
            pltpu.VMEM((2 * R, D_R), F32),
            pltpu.VMEM((2 * R, D_R), F32),
            pltpu.VMEM((R, D_R), F32),
            pltpu.VMEM((R, D_R), F32),
        ],
        input_output_aliases={5: 2, 6: 1},
        compiler_params=_params("arbitrary"),
        name="wkv_sample",
    )(z, z, z, z, z, shift_buf, state_buf, vecs, w2a2, tri, pmat, gmask)


def _layer_norm(x, g, b):
    mu = jnp.mean(x, axis=-1, keepdims=True)
    d = x - mu
    var = jnp.mean(d * d, axis=-1, keepdims=True)
    return d * lax.rsqrt(var + LN_EPS) * g + b


def _sgu_prompt_kernel(u_ref, vg_ref, gg_ref, lng_ref, lnb_ref, w_ref, bias_ref, y_ref):
    L = SGU_L
    vn = _layer_norm(vg_ref[...].astype(F32), lng_ref[...], lnb_ref[...]).astype(BF16)
    causal = _iota((L, L), 0) >= _iota((L, L), 1)
    wm = [jnp.where(causal, w_ref[g], 0.0).astype(BF16) for g in range(N_GROUPS)]
    rows = []
    for c in range(SGU_CHUNKS):
        parts = [_dot(wm[g], vn[c * L:(c + 1) * L, g * GW:(g + 1) * GW]) for g in range(N_GROUPS)]
        rows.append(jnp.concatenate(parts, axis=1) + bias_ref[...])
    s = jnp.concatenate(rows, axis=0)
    gg = gg_ref[...].astype(F32)
    y_ref[...] = (u_ref[...].astype(F32) * s * (gg * jax.nn.sigmoid(gg))).astype(BF16)


def _sgu_prompt(z, ln_g, ln_b, sgu_w, bias_exp, layer):
    L = SGU_L
    rows = L * SGU_CHUNKS
    zspec = lambda cb: pl.BlockSpec((rows, D_G), lambda i: (i, cb))
    vspec = pl.BlockSpec((None, 1, D_G), lambda i: (layer, 0, 0))
    return pl.pallas_call(
        _sgu_prompt_kernel,
        grid=(N_PROMPT // rows,),
        in_specs=[
            zspec(CB_U), zspec(CB_VG), zspec(CB_GG), vspec, vspec,
            pl.BlockSpec((None, N_GROUPS, L, L), lambda i: (layer, 0, 0, 0)),
            pl.BlockSpec((None, L, D_G), lambda i: (layer, 0, 0)),
        ],
        out_specs=pl.BlockSpec((rows, D_G), lambda i: (i, 0)),
        out_shape=jax.ShapeDtypeStruct((N_PROMPT, D_G), BF16),
        compiler_params=_params("arbitrary"),
        name="sgu_prompt",
    )(z, z, z, ln_g, ln_b, sgu_w, bias_exp)


def _sgu_sample_kernel(u_ref, vg_ref, gg_ref, lng_ref, lnb_ref, wc_ref, bias_ref, *rest):
    y_ref, vn_ref = rest[-2:]
    T = DEC_SEQ
    R = u_ref.shape[0]
    nb = R // T
    vn = _layer_norm(vg_ref[...].astype(F32), lng_ref[...], lnb_ref[...])
    vn_ref[...] = vn
    vn3 = vn.reshape(nb, T, D_G)
    s3 = jnp.broadcast_to(bias_ref[...][None], (nb, T, D_G))
    for j in range(T):
        s3 = s3 + jnp.broadcast_to(vn3[:, j:j + 1, :], vn3.shape) * wc_ref[j][None]
    gg = gg_ref[...].astype(F32)
    y_ref[...] = (u_ref[...].astype(F32) * s3.reshape(R, D_G) * (gg * jax.nn.sigmoid(gg))).astype(BF16)


def _sgu_sample(z, ln_g, ln_b, wc, bias_c, vn_buf, layer):
    R = 128
    zspec = lambda cb: pl.BlockSpec((R, D_G), lambda i: (i, cb))
    vspec = pl.BlockSpec((None, 1, D_G), lambda i: (layer, 0, 0))
    extra_specs, extra_ops, aliases = _stack_io((vn_buf,), 7, 1)
    return pl.pallas_call(
        _sgu_sample_kernel,
        grid=(N_SAMPLE // R,),
        in_specs=[
            zspec(CB_U), zspec(CB_VG), zspec(CB_GG), vspec, vspec,
            pl.BlockSpec((None, DEC_SEQ, DEC_SEQ, D_G), lambda i: (layer, 0, 0, 0)),
            pl.BlockSpec((None, DEC_SEQ, D_G), lambda i: (layer, 0, 0)),
        ] + extra_specs,
        out_specs=[pl.BlockSpec((R, D_G), lambda i: (i, 0)),
                   pl.BlockSpec((None, R, D_G), lambda i: (layer, i, 0))],
        out_shape=[jax.ShapeDtypeStruct((N_SAMPLE, D_G), BF16),
                   jax.ShapeDtypeStruct((DEPTH, N_SAMPLE, D_G), F32)],
        input_output_aliases=aliases,
        compiler_params=_params("arbitrary"),
        name="sgu_sample",
    )(z, z, z, ln_g, ln_b, wc, bias_c, *extra_ops)


def _outproj_kernel(ya_ref, yb_ref, ga_ref, gb_ref, h_ref, wa_ref, wb_ref, wo_ref, gn_ref, *outs):
    ya = _dot(ya_ref[...], wa_ref[...])
    yb = _dot(yb_ref[...], wb_ref[...])
    m = jax.nn.sigmoid(ga_ref[...].astype(F32)) * ya + jax.nn.sigmoid(gb_ref[...].astype(F32)) * yb
    h_new = h_ref[...] + _dot(m.astype(BF16), wo_ref[...])
    xn = _rms_scale(h_new, gn_ref[...])
    if len(outs) == 1:
        outs[0][...] = xn
    else:
        outs[0][...] = h_new
        outs[1][...] = xn.astype(BF16)


def _outproj(ya_in, yb_in, z, h, wpa, wpb, wout, gn, layer, last):
    n_tok = h.shape[0]
    tm = TM_OUT
    wspec = lambda a: pl.BlockSpec((None,) + a.shape[1:], lambda i: (layer, 0, 0),
                                   pipeline_mode=pl.Buffered(1))
    row = lambda w: pl.BlockSpec((tm, w), lambda i: (i, 0))
    tok = lambda dt: jax.ShapeDtypeStruct((n_tok, D_MODEL), dt)
    return pl.pallas_call(
        _outproj_kernel,
        grid=(n_tok // tm,),
        in_specs=[
            row(D_R), row(D_G), row(D_MODEL),
            pl.BlockSpec((tm, D_MODEL), lambda i: (i, 1)),
            row(D_MODEL),
            wspec(wpa), wspec(wpb), wspec(wout),
            pl.BlockSpec((None, 1, D_MODEL), lambda i: (gn[1], 0, 0)),
        ],
        out_specs=[row(D_MODEL)] if last else [row(D_MODEL), row(D_MODEL)],
        out_shape=[tok(F32)] if last else [tok(F32), tok(BF16)],
        compiler_params=_params("arbitrary"),
        name="outproj",
    )(ya_in, yb_in, z, z, h, wpa, wpb, wout, gn[0])


def _block_tri(n, blk):
    i = jnp.arange(n)
    return ((i[:, None] >= i[None, :]) & (i[:, None] // blk == i[None, :] // blk)).astype(BF16)


def kernel(x_prompt, x_sample, state_wkv, state_shift, norm_g, w_in, shift_mu, w0, w2, a0, a2, k_k, k_a, r_k,
           lnx_g, lnx_b, sgu_ln_g, sgu_ln_b, sgu_w, sgu_b, w_proj_a, w_proj_b, w_out, final_norm_g):
    c_wd, c_gr = 3 * D_R, D_SHIFT
    wpa, wpb, wout = w_proj_a.astype(BF16), w_proj_b.astype(BF16), w_out.astype(BF16)

    zrow = jnp.zeros((DEPTH, D_R), F32)
    mu_wa = jnp.pad(shift_mu[:, c_wd:c_gr], ((0, 0), (0, D_R - 2 * R_W)))
    vec_rows = [shift_mu[:, 0:D_R], shift_mu[:, D_R:2 * D_R], shift_mu[:, 2 * D_R:3 * D_R], w0, a0, k_k, k_a,
                r_k.reshape(DEPTH, D_R), lnx_g, lnx_b, mu_wa] + [zrow] * (N_VEC - 11)
    vecs = jnp.stack(vec_rows, axis=1)
    zblk = jnp.zeros((DEPTH, R_W, D_R), F32)
    w2a2 = jnp.concatenate([jnp.concatenate([w2, zblk], axis=2),
                            jnp.concatenate([zblk, a2], axis=2)], axis=1).astype(BF16)
    norm_g3 = norm_g.reshape(DEPTH, 1, D_MODEL)
    final_g3 = final_norm_g.reshape(1, 1, D_MODEL)
    ln_g3, ln_b3 = sgu_ln_g.reshape(DEPTH, 1, D_G), sgu_ln_b.reshape(DEPTH, 1, D_G)
    bias_exp = jnp.repeat(jnp.swapaxes(sgu_b, 1, 2), GW, axis=2)
    tmask = jnp.tril(jnp.ones((DEC_SEQ, DEC_SEQ), F32))
    wc = jnp.repeat(jnp.transpose(sgu_w[:, :, :DEC_SEQ, :DEC_SEQ] * tmask, (0, 3, 2, 1)), GW, axis=3)
    bias_c = bias_exp[:, :DEC_SEQ, :]

    pmat = ((jnp.arange(GL)[:, None] // HEAD) == (jnp.arange(GL)[None, :] // HEAD)).astype(BF16)
    tri_p = _block_tri(WKV_L, WKV_L)
    tri_s = _block_tri(SAMPLE_NB * DEC_SEQ, DEC_SEQ)
    src_cols = jnp.asarray(Z_SRC_UNITS, jnp.int32)

    h_p = x_prompt.reshape(N_PROMPT, D_MODEL)
    h_s = x_sample.reshape(N_SAMPLE, D_MODEL)
    xn_p = xn_s = None
    stk_p = [jnp.zeros((DEPTH, BATCH, H_R, HEAD, HEAD), F32), jnp.zeros((DEPTH, BATCH, 1, D_SHIFT), F32)]
    wkv_s, shift_s = state_wkv, state_shift
    vn_s = jnp.zeros((DEPTH, N_SAMPLE, D_G), F32)
    for l in range(DEPTH):
        last = l == DEPTH - 1
        if l == 0:
            z_p = _inproj(h_p, w_in, src_cols, l, TM_IN_NORM, norm_g3)
            z_s = _inproj(h_s, w_in, src_cols, l, TM_IN_SAMPLE, norm_g3)
        else:
            z_p = _inproj(xn_p, w_in, src_cols, l, TM_IN_PROMPT)
            z_s = _inproj(xn_s, w_in, src_cols, l, TM_IN_SAMPLE)
        ya_p, *stk_p = _wkv_prompt(z_p.reshape(BATCH, SEQ, NZ), vecs, w2a2, tri_p, pmat, pmat, stk_p, l)
        ya_s, wkv_s, shift_s = _wkv_sample(z_s, shift_s, wkv_s, vecs, w2a2, tri_s, pmat, pmat, l)
        yb_p = _sgu_prompt(z_p, ln_g3, ln_b3, sgu_w, bias_exp, l)
        yb_s, vn_s = _sgu_sample(z_s, ln_g3, ln_b3, wc, bias_c, vn_s, l)
        gn = (final_g3, 0) if last else (norm_g3, l + 1)
        out_p = _outproj(ya_p.reshape(N_PROMPT, D_R), yb_p, z_p, h_p, wpa, wpb, wout, gn, l, last)
        out_s = _outproj(ya_s, yb_s, z_s, h_s, wpa, wpb, wout, gn, l, last)
        if last:
            y_p, y_s = out_p[0], out_s[0]
        else:
            (h_p, xn_p), (h_s, xn_s) = out_p, out_s
    return (y_p.reshape(BATCH, SEQ, D_MODEL), y_s.reshape(DEC_BATCH, DEC_SEQ, D_MODEL),
            stk_p[0], stk_p[1], wkv_s, shift_s, vn_s.reshape(DEPTH, DEC_BATCH, DEC_SEQ, D_G))
```

```python
import math

import jax
import jax.numpy as jnp
from jax import lax
from jax.experimental import pallas as pl
from jax.experimental.pallas import tpu as pltpu

F32 = jnp.float32
BF16 = jnp.bfloat16

D_MODEL = 2048
BATCH = 4
SEQ = 2048
DEPTH = 4
DEC_BATCH = 128
DEC_SEQ = 8
HEAD = 64
D_R = 1024
H_R = 16
R_W = 64
R_A = 64
D_G = 1024
N_GROUPS = 8
GW = 128
D_SHIFT = 3 * D_R + R_W + R_A
RMS_EPS = 1e-6
LN_EPS = 1e-5
GN_EPS = 64e-5
EXP_M05 = math.exp(-0.5)

N_PROMPT = BATCH * SEQ
N_SAMPLE = DEC_BATCH * DEC_SEQ

LANE = 128
CB_U, CB_VG, CB_GG, CB_R, CB_K, CB_V, CB_GR = 4, 5, 6, 7, 8, 9, 10
CB_WA = 88
TN_IN = 512
NZ = 23 * TN_IN
Z_SRC_UNITS = tuple([57 + 4 * j for j in range(8)] + [33 + 4 * j for j in range(6)] + [4 * j for j in range(6)]
                    + [25, 29] + [24])

HG = 4
GL = HG * HEAD
N_HG = H_R // HG

WKV_L = 64
N_CHAIN = BATCH * N_HG
SAMPLE_NB = 8

TM_IN_PROMPT, TM_IN_SAMPLE = 4096, 1024
TM_IN_NORM = 2048
TM_OUT = 512
SGU_L = 128
VMEM_LIMIT = 56 * 1024 * 1024
VMEM_LIMIT_OUT = 60 * 1024 * 1024

V_MU_R, V_MU_K, V_MU_V, V_W0, V_A0, V_KK, V_KA, V_RK, V_LNG, V_LNB, V_MU_WA = range(11)
N_VEC = 16


def _dot(a, b):
    return jnp.dot(a, b, preferred_element_type=F32)


def _mm(a, b):
    return _dot(a.astype(BF16), b.astype(BF16))


def _mm_nt(a, b):
    return lax.dot_general(a.astype(BF16), b.astype(BF16), (((1,), (1,)), ((), ())),
                           preferred_element_type=F32)


def _mm_tn(a, b):
    return lax.dot_general(a.astype(BF16), b.astype(BF16), (((0,), (0,)), ((), ())),
                           preferred_element_type=F32)


def _iota(shape, dim):
    return lax.broadcasted_iota(jnp.int32, shape, dim)


def _block_diag(x, gmask_bf):
    reps = GL // x.shape[0]
    return jnp.concatenate([x.astype(BF16)] * reps, axis=0) * gmask_bf


def _segsum(x, pmat):
    rows, n = x.shape[0], x.shape[1] // GL
    xb = x.astype(BF16)
    stacked = jnp.concatenate([xb[:, g * GL:(g + 1) * GL] for g in range(n)], axis=0)
    s = _dot(stacked, pmat)
    return jnp.concatenate([s[g * rows:(g + 1) * rows] for g in range(n)], axis=1)


def _cumsum_rows(tri, x):
    hi = x.astype(BF16)
    lo = (x - hi.astype(F32)).astype(BF16)
    return _dot(tri, hi) + _dot(tri, lo)


def _rms_scale(x, g):
    ms = jnp.mean(x * x, axis=-1, keepdims=True)
    return x * lax.rsqrt(ms + RMS_EPS) * g


def _params(*sem):
    return pltpu.CompilerParams(dimension_semantics=sem, vmem_limit_bytes=VMEM_LIMIT)


def _inproj_kernel(cols_ref, x_ref, w_ref, z_ref):
    del cols_ref
    z_ref[...] = _dot(x_ref[...], w_ref[...].astype(BF16)).astype(z_ref.dtype)


def _inproj_norm_kernel(cols_ref, x_ref, g_ref, w_ref, z_ref, xn_ref):
    del cols_ref

    @pl.when(pl.program_id(1) == 0)
    def _():
        xn_ref[...] = _rms_scale(x_ref[...], g_ref[...]).astype(BF16)

    z_ref[...] = _dot(xn_ref[...], w_ref[...].astype(BF16)).astype(z_ref.dtype)


def _inproj(x, w_in, src_cols, layer, tm, norm_g3=None):
    n_tok = x.shape[0]
    wspec = pl.BlockSpec((None, pl.Element(D_MODEL), pl.Element(TN_IN)),
                         lambda i, j, units: (layer, 0, units[j] * LANE))
    if norm_g3 is None:
        kern, scratch, operands = _inproj_kernel, [], (x, w_in)
        in_specs = [pl.BlockSpec((tm, D_MODEL), lambda i, j, units: (i, 0)), wspec]
    else:
        kern, scratch, operands = _inproj_norm_kernel, [pltpu.VMEM((tm, D_MODEL), BF16)], (x, norm_g3, w_in)
        in_specs = [pl.BlockSpec((tm, D_MODEL), lambda i, j, units: (i, 0), pipeline_mode=pl.Buffered(1)),
                    pl.BlockSpec((None, 1, D_MODEL), lambda i, j, units: (layer, 0, 0)), wspec]
    grid_spec = pltpu.PrefetchScalarGridSpec(
        num_scalar_prefetch=1,
        grid=(n_tok // tm, NZ // TN_IN),
        in_specs=in_specs,
        out_specs=pl.BlockSpec((tm, TN_IN), lambda i, j, units: (i, j)),
        scratch_shapes=scratch,
    )
    return pl.pallas_call(
        kern,
        grid_spec=grid_spec,
        out_shape=jax.ShapeDtypeStruct((n_tok, NZ), BF16),
        compiler_params=_params("arbitrary", "arbitrary"),
        name="inproj",
    )(src_cols, *operands)


def _shift_mix(x, x_prev, mu):
    return x + (x_prev - x) * mu


def _lowrank_in(xwa):
    return jnp.where(_iota(xwa.shape, 1) < R_W, jnp.tanh(xwa), xwa).astype(BF16)


def _prep_gates(xr, xk, lr_w, lr_a, vecs):
    vec = lambda i: vecs[i:i + 1, :]
    lw = -EXP_M05 * jax.nn.sigmoid(vec(V_W0) + lr_w)
    ag = jax.nn.sigmoid(vec(V_A0) + lr_a)
    kk = xk * vec(V_KK)
    k2 = xk * (1.0 + (ag - 1.0) * vec(V_KA))
    return dict(lw=lw, ag=ag, kk=kk, k2=k2, seg_in=jnp.concatenate([kk * kk, xr * k2 * vec(V_RK)], axis=0))


def _prep_scale(xr, xv, gt, cl, seg, last_row):
    rows = cl.shape[0]
    kkn = gt["kk"] / jnp.maximum(jnp.sqrt(seg[:rows]), 1e-12)
    cl_last = last_row(cl)
    e_neg = jnp.exp(-cl)
    e_rem = jnp.exp(cl_last - cl)
    bb = kkn * gt["ag"]
    return dict(
        rt=xr * jnp.exp(cl),
        kt=gt["k2"] * e_neg,
        at=-kkn * jnp.exp(cl - gt["lw"]),
        bt=bb * e_neg,
        kd=gt["k2"] * e_rem,
        bd=bb * e_rem,
        v=xv,
        wl=jnp.exp(cl_last),
        bonus=seg[rows:] * xv,
    )


def _wkv_prep(r, k, v, wa, r_prev, k_prev, v_prev, wa_prev, vecs, w2a2, tri, pmat, last_row):
    vec = lambda i: vecs[i:i + 1, :]
    xr = _shift_mix(r, r_prev, vec(V_MU_R))
    xk = _shift_mix(k, k_prev, vec(V_MU_K))
    xv = _shift_mix(v, v_prev, vec(V_MU_V))
    lowrank = _dot(_lowrank_in(_shift_mix(wa, wa_prev, vecs[V_MU_WA:V_MU_WA + 1, :2 * R_W])), w2a2)
    gt = _prep_gates(xr, xk, lowrank[:, :D_R], lowrank[:, D_R:], vecs)
    return _prep_scale(xr, xv, gt, _cumsum_rows(tri, gt["lw"]), _segsum(gt["seg_in"], pmat), last_row)


def _wkv_finish(y, bonus, g_r, vecs, pmat):
    mu = _segsum(y, pmat) * (1.0 / HEAD)
    d = y - mu
    var = _segsum(d * d, pmat) * (1.0 / HEAD)
    yn = d * lax.rsqrt(var + GN_EPS) * vecs[V_LNG:V_LNG + 1, :] + vecs[V_LNB:V_LNB + 1, :]
    return ((yn + bonus) * (g_r * jax.nn.sigmoid(g_r))).astype(BF16)


def _grouped(x, nb):
    return x.reshape(nb, x.shape[0] // nb, x.shape[-1])


def _bcast_row(x, nb, j):
    x3 = _grouped(x, nb)
    return jnp.broadcast_to(x3[:, j:j + 1, :], x3.shape).reshape(x.shape)


def _prev_rows(x, prev, nb):
    t = x.shape[0] // nb
    first = (_iota(x.shape, 0) & (t - 1)) == 0
    return jnp.where(first, jnp.broadcast_to(prev, (nb, t, x.shape[-1])).reshape(x.shape), pltpu.roll(x, 1, 0))


def _interleave(major, minor):
    done = 0
    for i, piece in enumerate(major):
        piece()
        upto = (i + 1) * len(minor) // len(major)
        for other in minor[done:upto]:
            other()
        done = upto
    for other in minor[done:]:
        other()


def _chunk_buffers():
    L = WKV_L
    return [
        pltpu.VMEM((N_CHAIN, 2 * L, GL), BF16),
        pltpu.VMEM((N_CHAIN, L, GL), BF16),
        pltpu.VMEM((N_CHAIN, 2 * L, GL), F32),
        pltpu.VMEM((N_CHAIN, L, GL), BF16),
        pltpu.VMEM((N_CHAIN, 2 * L, GL), BF16),
        pltpu.VMEM((BATCH * L, D_R), F32),
        pltpu.VMEM((BATCH * L, D_R), F32),
        pltpu.VMEM((BATCH, 8, D_R), F32),
    ]


N_CHUNK_BUF = 8


def _wkv_prompt_kernel(r_ref, k_ref, v_ref, wa_ref, gr_ref, vecs_ref, w2a2_ref, tri_ref, pmat_ref, gmask_ref,
                       *rest):
    n_scr = 6 + 2 * N_CHUNK_BUF
    y_ref, s_out_ref, sh_out_ref = rest[-n_scr - 3:-n_scr]
    s_ref, pr_ref, pk_ref, pv_ref, pwa_ref, y_s = rest[-n_scr:-n_scr + 6]
    set_a = rest[-2 * N_CHUNK_BUF:-N_CHUNK_BUF]
    set_b = rest[-N_CHUNK_BUF:]
    prevs = (pr_ref, pk_ref, pv_ref, pwa_ref)
    s = pl.program_id(0)

    @pl.when(s == 0)
    def _():
        s_ref[...] = jnp.zeros_like(s_ref)
        for ref in prevs + tuple(set_b):
            ref[...] = jnp.zeros_like(ref)

    args = (r_ref, k_ref, v_ref, wa_ref, gr_ref, vecs_ref, w2a2_ref, tri_ref, pmat_ref, gmask_ref,
            y_ref, s_ref, prevs, y_s)

    @pl.when((s & 1) == 0)
    def _():
        _wkv_prompt_step(*args, set_a, set_b)

    @pl.when((s & 1) == 1)
    def _():
        _wkv_prompt_step(*args, set_b, set_a)

    @pl.when(s == pl.num_programs(0) - 1)
    def _():
        for b in range(BATCH):
            for g in range(N_HG):
                for j in range(HG):
                    s_out_ref[b, g * HG + j] = s_ref[b, g, j * HEAD:(j + 1) * HEAD, :][:, j * HEAD:(j + 1) * HEAD]
        sh_out_ref[:, :, 0:D_R] = pr_ref[...]
        sh_out_ref[:, :, D_R:2 * D_R] = pk_ref[...]
        sh_out_ref[:, :, 2 * D_R:3 * D_R] = pv_ref[...]
        sh_out_ref[:, :, 3 * D_R:D_SHIFT] = pwa_ref[...]


def _wkv_prompt_step(r_ref, k_ref, v_ref, wa_ref, gr_ref, vecs_ref, w2a2_ref, tri_ref, pmat_ref, gmask_ref,
                     y_ref, s_ref, prevs, y_s, wset, rset):
    L, nb = WKV_L, BATCH
    vecs = vecs_ref[...]
    pmat = pmat_ref[...]
    gmask = gmask_ref[...]
    gmask_f = gmask.astype(F32)
    tri = tri_ref[...]
    rs = lambda b: slice(b * L, (b + 1) * L)
    sl = lambda g: slice(g * GL, (g + 1) * GL)
    chains = [(b, g) for b in range(nb) for g in range(N_HG)]
    lhs_r, p_r, av_r, arb_r, zd_r, v_r, bonus_r, wl_r = rset
    lhs_w, p_w, av_w, arb_w, zd_w, v_w, bonus_w, wl_w = wset
    pr_ref, pk_ref, pv_ref, pwa_ref = prevs

    st = {}

    def b_state_terms(n, b, g):
        st["gs", n] = _mm_nt(lhs_r[n], s_ref[b, g])

    def b_u(n, b, g):
        st["u", n] = _dot(p_r[n], _block_diag(st["gs", n][:L] + av_r[n, 0:L, :], gmask))

    def b_y(n, b, g):
        y_s[rs(b), sl(g)] = st["gs", n][L:] + av_r[n, L:2 * L, :] + _dot(arb_r[n], _block_diag(st["u", n], gmask))

    def b_state(n, b, g):
        uv = jnp.concatenate([st["u", n], v_r[rs(b), sl(g)]], axis=0)
        s_ref[b, g] = (s_ref[b, g] * wl_r[b, 0:1, sl(g)] + _mm_tn(uv, zd_r[n])) * gmask_f

    def staged(fns, members):
        return [(lambda f=f, n=n, b=b, g=g: f(n, b, g)) for f in fns for n, (b, g) in members]

    state_part = staged([b_state_terms, b_u, b_y, b_state], list(enumerate(chains)))

    row = _iota((L, GL), 0)
    col = _iota((L, GL), 1) & (HEAD - 1)
    strict = row > col
    incl = row >= col
    row2 = _iota((2 * L, GL), 0)
    col2 = _iota((2 * L, GL), 1) & (HEAD - 1)
    both = (row2 - jnp.where(row2 < L, 0, L - 1)) > col2
    eye = jnp.where(row == col, 1.0, 0.0)
    pp = {}

    def prep_pieces(b):
        def mix_r():
            x = r_ref[b].astype(F32)
            pp["xr", b] = _shift_mix(x, _prev_rows(x, pr_ref[b:b + 1], 1), vecs[V_MU_R:V_MU_R + 1, :])
            pr_ref[b] = x[L - 1:L, :]

        def mix_k():
            x = k_ref[b].astype(F32)
            pp["xk", b] = _shift_mix(x, _prev_rows(x, pk_ref[b:b + 1], 1), vecs[V_MU_K:V_MU_K + 1, :])
            pk_ref[b] = x[L - 1:L, :]

        def mix_v():
            x = v_ref[b].astype(F32)
            xv = _shift_mix(x, _prev_rows(x, pv_ref[b:b + 1], 1), vecs[V_MU_V:V_MU_V + 1, :])
            pp["xv", b] = xv
            v_w[rs(b), :] = xv
            pv_ref[b] = x[L - 1:L, :]

        def lowrank():
            x = wa_ref[b].astype(F32)
            xwa = _shift_mix(x, _prev_rows(x, pwa_ref[b:b + 1], 1), vecs[V_MU_WA:V_MU_WA + 1, :2 * R_W])
            pwa_ref[b] = x[L - 1:L, :]
            pp["lr", b] = _dot(_lowrank_in(xwa), w2a2_ref[...])

        def gates(g):
            def run():
                xr, xk = pp["xr", b][:, sl(g)], pp["xk", b][:, sl(g)]
                lr = pp["lr", b]
                gt = _prep_gates(xr, xk, lr[:, sl(g)], lr[:, D_R + g * GL:D_R + (g + 1) * GL], vecs[:, sl(g)])
                pp["gt", b, g] = gt
                pp["cl", b, g] = _cumsum_rows(tri, gt["lw"])
                pp["seg", b, g] = _segsum(gt["seg_in"], pmat)
            return run

        def scale(g):
            def run():
                n = b * N_HG + g
                ops = _prep_scale(pp["xr", b][:, sl(g)], pp["xv", b][:, sl(g)], pp["gt", b, g], pp["cl", b, g],
                                  pp["seg", b, g], lambda x: jnp.broadcast_to(x[L - 1:L, :], x.shape))
                bonus_w[rs(b), sl(g)] = ops["bonus"]
                wl_w[b, :, sl(g)] = ops["wl"][0:8, :]
                lhs = jnp.concatenate([ops["at"], ops["rt"]], axis=0).astype(BF16)
                lhs_w[n] = lhs
                zd_w[n] = jnp.concatenate([ops["bd"], ops["kd"]], axis=0).astype(BF16)
                pp["lhs", n] = lhs
                pp["bk", n] = jnp.concatenate([_block_diag(ops["bt"], gmask), _block_diag(ops["kt"], gmask)],
                                              axis=0)
            return run

        return ([mix_r, mix_k, mix_v, lowrank] + [gates(g) for g in range(N_HG)]
                + [scale(g) for g in range(N_HG)])

    def a_scores(n, b, g):
        sc = _mm_nt(pp["lhs", n], pp["bk", n])
        a_ab = jnp.where(strict, sc[:L, :GL], 0.0)
        arb_w[n] = jnp.where(incl, sc[L:, :GL], 0.0).astype(BF16)
        pp["akv", n] = jnp.where(both, sc[:, GL:], 0.0).astype(BF16)
        pp["x", n] = a_ab.astype(BF16)
        pp["xbd", n] = _block_diag(a_ab, gmask)
        pp["p", n] = eye + a_ab

    def a_values(n, b, g):
        av_w[n] = _dot(pp["akv", n], _block_diag(pp["xv", b][:, sl(g)], gmask))

    def a_square(n, b, g):
        x = _dot(pp["x", n], pp["xbd", n]).astype(BF16)
        pp["x", n], pp["xbd", n] = x, _block_diag(x, gmask)

    def a_double(n, b, g):
        xp = _dot(jnp.concatenate([pp["x", n], pp["p", n].astype(BF16)], axis=0), pp["xbd", n])
        x = xp[:L].astype(BF16)
        pp["x", n], pp["xbd", n] = x, _block_diag(x, gmask)
        pp["p", n] = pp["p", n] + xp[L:]

    def a_inverse(n, b, g):
        p_w[n] = (pp["p", n] + _dot(pp["p", n].astype(BF16), pp["xbd", n])).astype(BF16)

    inv_stages = [a_scores, a_values, a_square] + [a_double] * (int(math.log2(L)) - 2) + [a_inverse]
    half = N_CHAIN // 2
    members = list(enumerate(chains))

    fin = {}

    def fin_mean(g):
        def run():
            y = y_s[:, sl(g)]
            fin[g] = y - _segsum(y, pmat) * (1.0 / HEAD)
        return run

    def fin_out(g):
        def run():
            d = fin[g]
            var = _segsum(d * d, pmat) * (1.0 / HEAD)
            yn = d * lax.rsqrt(var + GN_EPS) * vecs[V_LNG:V_LNG + 1, sl(g)] + vecs[V_LNB:V_LNB + 1, sl(g)]
            g_r = gr_ref[:, :, sl(g)].astype(F32).reshape(nb * L, GL)
            out = (yn + bonus_r[:, sl(g)]) * (g_r * jax.nn.sigmoid(g_r))
            y_ref[:, :, sl(g)] = out.astype(BF16).reshape(nb, L, GL)
        return run

    _interleave(state_part, prep_pieces(0) + prep_pieces(1))
    _interleave(staged(inv_stages, members[:half]), prep_pieces(2) + prep_pieces(3))
    _interleave(staged(inv_stages, members[half:]),
                [fin_mean(g) for g in range(N_HG)] + [fin_out(g) for g in range(N_HG)])


def _stack_io(bufs, n_in, first_out):
    return ([pl.BlockSpec(memory_space=pl.ANY)] * len(bufs), list(bufs),
            {n_in + i: first_out + i for i in range(len(bufs))})


def _wkv_prompt(z3, vecs, w2a2, tri, pmat, gmask, stacked, layer):
    L = WKV_L
    n_chunks = SEQ // L
    cur = lambda s: jnp.minimum(s, n_chunks - 1)
    prv = lambda s: jnp.maximum(s - 1, 0)
    zspec = lambda cb: pl.BlockSpec((BATCH, L, D_R), lambda s: (0, cur(s), cb))
    full = lambda a: pl.BlockSpec(a.shape, lambda s: (0,) * a.ndim)
    row_scr = lambda w: pltpu.VMEM((BATCH, 1, w), F32)
    extra_specs, extra_ops, aliases = _stack_io(stacked, 10, 1)
    return pl.pallas_call(
        _wkv_prompt_kernel,
        grid=(n_chunks + 1,),
        in_specs=[
            zspec(CB_R), zspec(CB_K), zspec(CB_V),
            pl.BlockSpec((BATCH, L, 2 * R_W), lambda s: (0, cur(s), CB_WA)),
            pl.BlockSpec((BATCH, L, D_R), lambda s: (0, prv(s), CB_GR)),
            pl.BlockSpec((None, N_VEC, D_R), lambda s: (layer, 0, 0)),
            pl.BlockSpec((None, 2 * R_W, 2 * D_R), lambda s: (layer, 0, 0)),
            full(tri), full(pmat), full(gmask),
        ] + extra_specs,
        out_specs=[
            pl.BlockSpec((BATCH, L, D_R), lambda s: (0, prv(s), 0)),
            pl.BlockSpec((None, BATCH, H_R, HEAD, HEAD), lambda s: (layer, 0, 0, 0, 0)),
            pl.BlockSpec((None, BATCH, 1, D_SHIFT), lambda s: (layer, 0, 0, 0)),
        ],
        out_shape=[
            jax.ShapeDtypeStruct((BATCH, SEQ, D_R), BF16),
            jax.ShapeDtypeStruct((DEPTH, BATCH, H_R, HEAD, HEAD), F32),
            jax.ShapeDtypeStruct((DEPTH, BATCH, 1, D_SHIFT), F32),
        ],
        scratch_shapes=[
            pltpu.VMEM((BATCH, N_HG, GL, GL), F32),
            row_scr(D_R), row_scr(D_R), row_scr(D_R), row_scr(2 * R_W),
            pltpu.VMEM((BATCH * L, D_R), F32),
        ] + _chunk_buffers() + _chunk_buffers(),
        input_output_aliases=aliases,
        compiler_params=_params("arbitrary"),
        name="wkv_prompt",
    )(z3, z3, z3, z3, z3, vecs, w2a2, tri, pmat, gmask, *extra_ops)


def _wkv_sample_kernel(r_ref, k_ref, v_ref, gr_ref, wa_ref, sh_in_ref, s_in_ref,
                       vecs_ref, w2a2_ref, tri_ref, pmat_ref, gmask_ref,
                       y_ref, s_out_ref, sh_out_ref, seg_s, u_s, zd_s, ga_s, gr_s):
    nb, T = SAMPLE_NB, DEC_SEQ
    R = nb * T
    vecs = vecs_ref[...]
    pmat = pmat_ref[...]
    gmask = gmask_ref[...]
    bc = lambda x, j: _bcast_row(x, nb, j)

    r, k, v, wa = [ref[...].astype(F32) for ref in (r_ref, k_ref, v_ref, wa_ref)]
    cols = ((0, D_R), (D_R, 2 * D_R), (2 * D_R, 3 * D_R), (3 * D_R, D_SHIFT))
    prev = [_prev_rows(x, sh_in_ref[:, :, lo:hi], nb) for x, (lo, hi) in zip((r, k, v, wa), cols)]
    for x, (lo, hi) in zip((r, k, v, wa), cols):
        sh_out_ref[:, :, lo:hi] = _grouped(x, nb)[:, T - 1:T, :]
    ops = _wkv_prep(r, k, v, wa, *prev, vecs, w2a2_ref[...], tri_ref[...], pmat, lambda x: bc(x, T - 1))
    rt, kt, at, bt, vv = ops["rt"], ops["kt"], ops["at"], ops["bt"], ops["v"]
    zd_s[...] = jnp.concatenate([ops["bd"], ops["kd"]], axis=0)

    sl = lambda g: slice(g * GL, (g + 1) * GL)
    groups = [(b, g) for b in range(nb) for g in range(N_HG)]

    def score_piece(j):
        def run():
            btj, ktj = bc(bt, j), bc(kt, j)
            seg = _segsum(jnp.concatenate([at * btj, at * ktj, rt * btj, rt * ktj], axis=0), pmat)
            for q in range(4):
                seg_s[j, q] = seg[q * R:(q + 1) * R]
        return run

    def state_term_piece(b, g):
        def run():
            s4 = s_in_ref[b, g * HG:(g + 1) * HG].reshape(GL, HEAD)
            s_bd = jnp.concatenate([s4.astype(BF16)] * HG, axis=1) * gmask
            lhs = jnp.concatenate([at[b * T:(b + 1) * T, sl(g)], rt[b * T:(b + 1) * T, sl(g)]], axis=0)
            gs = _mm_nt(lhs, s_bd)
            ga_s[b * T:(b + 1) * T, sl(g)] = gs[:T]
            gr_s[b * T:(b + 1) * T, sl(g)] = gs[T:]
        return run

    _interleave([state_term_piece(b, g) for b, g in groups], [score_piece(j) for j in range(T)])

    t_idx = _iota((R, D_R), 0) & (T - 1)
    u = ga_s[...]
    for j in range(T - 1):
        u = u + jnp.where(t_idx > j, seg_s[j, 1] * bc(vv, j), 0.0)
    for j in range(T - 1):
        u = u + jnp.where(t_idx > j, seg_s[j, 0] * bc(u, j), 0.0)
    u_s[...] = jnp.concatenate([u, vv], axis=0)
    wl = ops["wl"]
    acc = {"y": gr_s[...]}

    def y_piece(j):
        def run():
            acc["y"] = acc["y"] + jnp.where(t_idx >= j, seg_s[j, 2] * bc(u, j) + seg_s[j, 3] * bc(vv, j), 0.0)
        return run

    def finish_piece():
        y_ref[...] = _wkv_finish(acc["y"], ops["bonus"], gr_ref[...].astype(F32), vecs, pmat)

    row2 = _iota((2 * R, GL), 0) & (R - 1)
    uv_t = {}

    def state_piece(g, b):
        def run():
            if b == 0:
                uv_t[g] = u_s[:, sl(g)].T.astype(BF16)
            mine = (row2 >= b * T) & (row2 < (b + 1) * T)
            upd = _dot(uv_t[g], jnp.where(mine, zd_s[:, sl(g)], 0.0).astype(BF16))
            for j in range(HG):
                h = g * HG + j
                s_out_ref[b, h] = (s_in_ref[b, h] * wl[b * T:b * T + 1, h * HEAD:(h + 1) * HEAD]
                                   + upd[j * HEAD:(j + 1) * HEAD, j * HEAD:(j + 1) * HEAD])
        return run

    _interleave([state_piece(g, b) for g in range(N_HG) for b in range(nb)],
                [y_piece(j) for j in range(T)] + [finish_piece])


def _wkv_sample(z, shift_buf, state_buf, vecs, w2a2, tri, pmat, gmask, layer):
    nb, T = SAMPLE_NB, DEC_SEQ
    R = nb * T
    zspec = lambda cb: pl.BlockSpec((R, D_R), lambda i: (i, cb))
    full = lambda a: pl.BlockSpec(a.shape, lambda i: (0,) * a.ndim)
    sspec = pl.BlockSpec((None, nb, H_R, HEAD, HEAD), lambda i: (layer, i, 0, 0, 0))
    shspec = pl.BlockSpec((None, nb, 1, D_SHIFT), lambda i: (layer, i, 0, 0))
    return pl.pallas_call(
        _wkv_sample_kernel,
        grid=(DEC_BATCH // nb,),
        in_specs=[
            zspec(CB_R), zspec(CB_K), zspec(CB_V), zspec(CB_GR),
            pl.BlockSpec((R, 2 * R_W), lambda i: (i, CB_WA)),
            shspec, sspec,
            pl.BlockSpec((None, N_VEC, D_R), lambda i: (layer, 0, 0)),
            pl.BlockSpec((None, 2 * R_W, 2 * D_R), lambda i: (layer, 0, 0)),
            full(tri), full(pmat), full(gmask),
        ],
        out_specs=[pl.BlockSpec((R, D_R), lambda i: (i, 0)), sspec, shspec],
        out_shape=[
            jax.ShapeDtypeStruct((N_SAMPLE, D_R), BF16),
            jax.ShapeDtypeStruct((DEPTH, DEC_BATCH, H_R, HEAD, HEAD), F32),
            jax.ShapeDtypeStruct((DEPTH, DEC_BATCH, 1, D_SHIFT), F32),
        ],
        scratch_shapes=[
            pltpu.VMEM((T, 4, R, D_R), F32),
            pltpu.VMEM((2 * R, D_R), F32),
            pltpu.VMEM((2 * R, D_R), F32),
            pltpu.VMEM((R, D_R), F32),
            pltpu.VMEM((R, D_R), F32),
        ],
        input_output_aliases={5: 2, 6: 1},
        compiler_params=_params("arbitrary"),
        name="wkv_sample",
    )(z, z, z, z, z, shift_buf, state_buf, vecs, w2a2, tri, pmat, gmask)


def _layer_norm(x, g, b):
    mu = jnp.mean(x, axis=-1, keepdims=True)
    d = x - mu
    var = jnp.mean(d * d, axis=-1, keepdims=True)
    return d * lax.rsqrt(var + LN_EPS) * g + b


def _sgu_sample_kernel(u_ref, vg_ref, gg_ref, lng_ref, lnb_ref, wc_ref, bias_ref, *rest):
    y_ref, vn_ref = rest[-2:]
    T = DEC_SEQ
    R = u_ref.shape[0]
    nb = R // T
    vn = _layer_norm(vg_ref[...].astype(F32), lng_ref[...], lnb_ref[...])
    vn_ref[...] = vn
    vn3 = vn.reshape(nb, T, D_G)
    s3 = jnp.broadcast_to(bias_ref[...][None], (nb, T, D_G))
    for j in range(T):
        s3 = s3 + jnp.broadcast_to(vn3[:, j:j + 1, :], vn3.shape) * wc_ref[j][None]
    gg = gg_ref[...].astype(F32)
    y_ref[...] = (u_ref[...].astype(F32) * s3.reshape(R, D_G) * (gg * jax.nn.sigmoid(gg))).astype(BF16)


def _sgu_sample(z, ln_g, ln_b, wc, bias_c, vn_buf, layer):
    R = 128
    zspec = lambda cb: pl.BlockSpec((R, D_G), lambda i: (i, cb))
    vspec = pl.BlockSpec((None, 1, D_G), lambda i: (layer, 0, 0))
    extra_specs, extra_ops, aliases = _stack_io((vn_buf,), 7, 1)
    return pl.pallas_call(
        _sgu_sample_kernel,
        grid=(N_SAMPLE // R,),
        in_specs=[
            zspec(CB_U), zspec(CB_VG), zspec(CB_GG), vspec, vspec,
            pl.BlockSpec((None, DEC_SEQ, DEC_SEQ, D_G), lambda i: (layer, 0, 0, 0)),
            pl.BlockSpec((None, DEC_SEQ, D_G), lambda i: (layer, 0, 0)),
        ] + extra_specs,
        out_specs=[pl.BlockSpec((R, D_G), lambda i: (i, 0)),
                   pl.BlockSpec((None, R, D_G), lambda i: (layer, i, 0))],
        out_shape=[jax.ShapeDtypeStruct((N_SAMPLE, D_G), BF16),
                   jax.ShapeDtypeStruct((DEPTH, N_SAMPLE, D_G), F32)],
        input_output_aliases=aliases,
        compiler_params=_params("arbitrary"),
        name="sgu_sample",
    )(z, z, z, ln_g, ln_b, wc, bias_c, *extra_ops)


def _outproj_tail(ya, yb_in, ga_ref, gb_ref, h_ref, wb_ref, wo_ref, gn_ref, outs):
    yb = _dot(yb_in, wb_ref[...])
    m = jax.nn.sigmoid(ga_ref[...].astype(F32)) * ya + jax.nn.sigmoid(gb_ref[...].astype(F32)) * yb
    h_new = h_ref[...] + _dot(m.astype(BF16), wo_ref[...])
    xn = _rms_scale(h_new, gn_ref[...])
    if len(outs) == 1:
        outs[0][...] = xn
    else:
        outs[0][...] = h_new
        outs[1][...] = xn.astype(BF16)


def _outproj_kernel(ya_ref, yb_ref, ga_ref, gb_ref, h_ref, wa_ref, wb_ref, wo_ref, gn_ref, *outs):
    _outproj_tail(_dot(ya_ref[...], wa_ref[...]), yb_ref[...], ga_ref, gb_ref, h_ref, wb_ref, wo_ref, gn_ref, outs)


def _outproj_sgu_kernel(ya_ref, u_ref, vg_ref, gg_ref, lng_ref, lnb_ref, sw_ref, sbias_ref,
                        ga_ref, gb_ref, h_ref, wa_ref, wb_ref, wo_ref, gn_ref, *outs):
    L = SGU_L
    n_col = 8
    cw = D_MODEL // n_col
    st = {}

    def ya_piece(c):
        def run():
            st["ya", c] = _dot(ya_ref[...], wa_ref[:, c * cw:(c + 1) * cw])
        return run

    causal = _iota((L, L), 0) >= _iota((L, L), 1)

    def sgu_norm(c):
        def run():
            st["vn", c] = _layer_norm(vg_ref[c * L:(c + 1) * L, :].astype(F32), lng_ref[...],
                                      lnb_ref[...]).astype(BF16)
        return run

    def sgu_mix(c):
        def run():
            parts = [_dot(jnp.where(causal, sw_ref[g], 0.0).astype(BF16), st["vn", c][:, g * GW:(g + 1) * GW])
                     for g in range(N_GROUPS)]
            gg = gg_ref[c * L:(c + 1) * L, :].astype(F32)
            st["yb", c] = (u_ref[c * L:(c + 1) * L, :].astype(F32) * (jnp.concatenate(parts, axis=1) + sbias_ref[...])
                           * (gg * jax.nn.sigmoid(gg))).astype(BF16)
        return run

    n_chunks = ya_ref.shape[0] // L
    _interleave([ya_piece(c) for c in range(n_col)],
                [f(c) for c in range(n_chunks) for f in (sgu_norm, sgu_mix)])
    ya = jnp.concatenate([st["ya", c] for c in range(n_col)], axis=1)
    yb_in = jnp.concatenate([st["yb", c] for c in range(n_chunks)], axis=0)
    _outproj_tail(ya, yb_in, ga_ref, gb_ref, h_ref, wb_ref, wo_ref, gn_ref, outs)


def _outproj(ya_in, yb_in, z, h, wpa, wpb, wout, gn, layer, last, sgu=None):
    n_tok = h.shape[0]
    tm = TM_OUT
    once = lambda shape, imap: pl.BlockSpec(shape, imap, pipeline_mode=pl.Buffered(1))
    wspec = lambda a: once((None,) + a.shape[1:], lambda i: (layer, 0, 0))
    row = lambda w: pl.BlockSpec((tm, w), lambda i: (i, 0))
    zcol = lambda w, cb: pl.BlockSpec((tm, w), lambda i: (i, cb))
    tok = lambda dt: jax.ShapeDtypeStruct((n_tok, D_MODEL), dt)
    vspec = pl.BlockSpec((None, 1, D_G), lambda i: (layer, 0, 0))
    if sgu is None:
        kern, branch_specs, branch_ops = _outproj_kernel, [row(D_G)], (yb_in,)
    else:
        kern = _outproj_sgu_kernel
        branch_specs = [zcol(D_G, CB_U), zcol(D_G, CB_VG), zcol(D_G, CB_GG), vspec, vspec,
                        once((None, N_GROUPS, SGU_L, SGU_L), lambda i: (layer, 0, 0, 0)),
                        once((None, SGU_L, D_G), lambda i: (layer, 0, 0))]
        branch_ops = (z, z, z) + tuple(sgu)
    return pl.pallas_call(
        kern,
        grid=(n_tok // tm,),
        in_specs=[row(D_R)] + branch_specs + [
            zcol(D_MODEL, 0), zcol(D_MODEL, 1), row(D_MODEL),
            wspec(wpa), wspec(wpb), wspec(wout),
            pl.BlockSpec((None, 1, D_MODEL), lambda i: (gn[1], 0, 0)),
        ],
        out_specs=[row(D_MODEL)] if last else [row(D_MODEL), row(D_MODEL)],
        out_shape=[tok(F32)] if last else [tok(F32), tok(BF16)],
        compiler_params=pltpu.CompilerParams(dimension_semantics=("arbitrary",), vmem_limit_bytes=VMEM_LIMIT_OUT),
        name="outproj",
    )(ya_in, *branch_ops, z, z, h, wpa, wpb, wout, gn[0])


def _block_tri(n, blk):
    i = jnp.arange(n)
    return ((i[:, None] >= i[None, :]) & (i[:, None] // blk == i[None, :] // blk)).astype(BF16)


def kernel(x_prompt, x_sample, state_wkv, state_shift, norm_g, w_in, shift_mu, w0, w2, a0, a2, k_k, k_a, r_k,
           lnx_g, lnx_b, sgu_ln_g, sgu_ln_b, sgu_w, sgu_b, w_proj_a, w_proj_b, w_out, final_norm_g):
    c_wd, c_gr = 3 * D_R, D_SHIFT
    wpa, wpb, wout = w_proj_a.astype(BF16), w_proj_b.astype(BF16), w_out.astype(BF16)

    zrow = jnp.zeros((DEPTH, D_R), F32)
    mu_wa = jnp.pad(shift_mu[:, c_wd:c_gr], ((0, 0), (0, D_R - 2 * R_W)))
    vec_rows = [shift_mu[:, 0:D_R], shift_mu[:, D_R:2 * D_R], shift_mu[:, 2 * D_R:3 * D_R], w0, a0, k_k, k_a,
                r_k.reshape(DEPTH, D_R), lnx_g, lnx_b, mu_wa] + [zrow] * (N_VEC - 11)
    vecs = jnp.stack(vec_rows, axis=1)
    zblk = jnp.zeros((DEPTH, R_W, D_R), F32)
    w2a2 = jnp.concatenate([jnp.concatenate([w2, zblk], axis=2),
                            jnp.concatenate([zblk, a2], axis=2)], axis=1).astype(BF16)
    norm_g3 = norm_g.reshape(DEPTH, 1, D_MODEL)
    final_g3 = final_norm_g.reshape(1, 1, D_MODEL)
    ln_g3, ln_b3 = sgu_ln_g.reshape(DEPTH, 1, D_G), sgu_ln_b.reshape(DEPTH, 1, D_G)
    bias_exp = jnp.repeat(jnp.swapaxes(sgu_b, 1, 2), GW, axis=2)
    tmask = jnp.tril(jnp.ones((DEC_SEQ, DEC_SEQ), F32))
    wc = jnp.repeat(jnp.transpose(sgu_w[:, :, :DEC_SEQ, :DEC_SEQ] * tmask, (0, 3, 2, 1)), GW, axis=3)
    bias_c = bias_exp[:, :DEC_SEQ, :]

    pmat = ((jnp.arange(GL)[:, None] // HEAD) == (jnp.arange(GL)[None, :] // HEAD)).astype(BF16)
    tri_p = _block_tri(WKV_L, WKV_L)
    tri_s = _block_tri(SAMPLE_NB * DEC_SEQ, DEC_SEQ)
    src_cols = jnp.asarray(Z_SRC_UNITS, jnp.int32)

    h_p = x_prompt.reshape(N_PROMPT, D_MODEL)
    h_s = x_sample.reshape(N_SAMPLE, D_MODEL)
    xn_p = xn_s = None
    stk_p = [jnp.zeros((DEPTH, BATCH, H_R, HEAD, HEAD), F32), jnp.zeros((DEPTH, BATCH, 1, D_SHIFT), F32)]
    wkv_s, shift_s = state_wkv, state_shift
    vn_s = jnp.zeros((DEPTH, N_SAMPLE, D_G), F32)
    for l in range(DEPTH):
        last = l == DEPTH - 1
        if l == 0:
            z_p = _inproj(h_p, w_in, src_cols, l, TM_IN_NORM, norm_g3)
            z_s = _inproj(h_s, w_in, src_cols, l, TM_IN_SAMPLE, norm_g3)
        else:
            z_p = _inproj(xn_p, w_in, src_cols, l, TM_IN_PROMPT)
            z_s = _inproj(xn_s, w_in, src_cols, l, TM_IN_SAMPLE)
        ya_p, *stk_p = _wkv_prompt(z_p.reshape(BATCH, SEQ, NZ), vecs, w2a2, tri_p, pmat, pmat, stk_p, l)
        ya_s, wkv_s, shift_s = _wkv_sample(z_s, shift_s, wkv_s, vecs, w2a2, tri_s, pmat, pmat, l)
        yb_s, vn_s = _sgu_sample(z_s, ln_g3, ln_b3, wc, bias_c, vn_s, l)
        gn = (final_g3, 0) if last else (norm_g3, l + 1)
        out_p = _outproj(ya_p.reshape(N_PROMPT, D_R), None, z_p, h_p, wpa, wpb, wout, gn, l, last,
                         sgu=(ln_g3, ln_b3, sgu_w, bias_exp))
        out_s = _outproj(ya_s, yb_s, z_s, h_s, wpa, wpb, wout, gn, l, last)
        if last:
            y_p, y_s = out_p[0], out_s[0]
        else:
            (h_p, xn_p), (h_s, xn_s) = out_p, out_s
    return (y_p.reshape(BATCH, SEQ, D_MODEL), y_s.reshape(DEC_BATCH, DEC_SEQ, D_MODEL),
            stk_p[0], stk_p[1], wkv_s, shift_s, vn_s.reshape(DEPTH, DEC_BATCH, DEC_SEQ, D_G))
```

```python
import math

import jax
import jax.numpy as jnp
from jax import lax
from jax.experimental import pallas as pl
from jax.experimental.pallas import tpu as pltpu

F32 = jnp.float32
BF16 = jnp.bfloat16

D_MODEL = 2048
BATCH = 4
SEQ = 2048
DEPTH = 4
DEC_BATCH = 128
DEC_SEQ = 8
HEAD = 64
D_R = 1024
H_R = 16
R_W = 64
R_A = 64
D_G = 1024
N_GROUPS = 8
GW = 128
D_SHIFT = 3 * D_R + R_W + R_A
RMS_EPS = 1e-6
LN_EPS = 1e-5
GN_EPS = 64e-5
EXP_M05 = math.exp(-0.5)

N_PROMPT = BATCH * SEQ
N_SAMPLE = DEC_BATCH * DEC_SEQ

LANE = 128
CB_U, CB_VG, CB_GG, CB_R, CB_K, CB_V, CB_GR = 4, 5, 6, 7, 8, 9, 10
CB_WA = 88
TN_IN = 512
NZ = 23 * TN_IN
Z_SRC_UNITS = tuple([57 + 4 * j for j in range(8)] + [33 + 4 * j for j in range(6)] + [4 * j for j in range(6)]
                    + [25, 29] + [24])

HG = 4
GL = HG * HEAD
N_HG = H_R // HG

WKV_L = 64
N_CHAIN = BATCH * N_HG
SAMPLE_NB = 8

TM_IN_PROMPT, TM_IN_SAMPLE = 4096, 1024
TM_IN_NORM = 2048
TM_OUT = 512
SGU_L = 128
VMEM_LIMIT = 56 * 1024 * 1024
VMEM_LIMIT_PROJ = 62 * 1024 * 1024

V_MU_R, V_MU_K, V_MU_V, V_W0, V_A0, V_KK, V_KA, V_RK, V_LNG, V_LNB, V_MU_WA = range(11)
N_VEC = 16


def _dot(a, b):
    return jnp.dot(a, b, preferred_element_type=F32)


def _mm(a, b):
    return _dot(a.astype(BF16), b.astype(BF16))


def _mm_nt(a, b):
    return lax.dot_general(a.astype(BF16), b.astype(BF16), (((1,), (1,)), ((), ())),
                           preferred_element_type=F32)


def _mm_tn(a, b):
    return lax.dot_general(a.astype(BF16), b.astype(BF16), (((0,), (0,)), ((), ())),
                           preferred_element_type=F32)


def _iota(shape, dim):
    return lax.broadcasted_iota(jnp.int32, shape, dim)


def _block_diag(x, gmask_bf):
    reps = GL // x.shape[0]
    return jnp.concatenate([x.astype(BF16)] * reps, axis=0) * gmask_bf


def _segsum(x, pmat):
    rows, n = x.shape[0], x.shape[1] // GL
    xb = x.astype(BF16)
    stacked = jnp.concatenate([xb[:, g * GL:(g + 1) * GL] for g in range(n)], axis=0)
    s = _dot(stacked, pmat)
    return jnp.concatenate([s[g * rows:(g + 1) * rows] for g in range(n)], axis=1)


def _cumsum_rows(tri, x):
    hi = x.astype(BF16)
    lo = (x - hi.astype(F32)).astype(BF16)
    return _dot(tri, hi) + _dot(tri, lo)


def _rms_scale(x, g):
    ms = jnp.mean(x * x, axis=-1, keepdims=True)
    return x * lax.rsqrt(ms + RMS_EPS) * g


def _params(*sem):
    return pltpu.CompilerParams(dimension_semantics=sem, vmem_limit_bytes=VMEM_LIMIT)


CAST_ROWS = 64
N_CAST_A = D_R // CAST_ROWS
N_CAST_O = D_MODEL // CAST_ROWS


def _inproj_body(x_bf, w_ref, z_ref, casts):
    z_ref[...] = _dot(x_bf, w_ref[...].astype(BF16)).astype(z_ref.dtype)
    if casts:
        wa_ref, wb_ref, wo_ref, wa_out, wb_out, wo_out = casts
        step = pl.program_id(0) * pl.num_programs(1) + pl.program_id(1)

        @pl.when(step < N_CAST_A)
        def _():
            wa_out[...] = wa_ref[...].astype(BF16)
            wb_out[...] = wb_ref[...].astype(BF16)

        @pl.when(step < N_CAST_O)
        def _():
            wo_out[...] = wo_ref[...].astype(BF16)


def _inproj_kernel(cols_ref, x_ref, w_ref, *rest):
    del cols_ref
    _inproj_body(x_ref[...], w_ref, rest[-4] if len(rest) > 1 else rest[0],
                 rest[:3] + rest[-3:] if len(rest) > 1 else None)


def _inproj_norm_kernel(cols_ref, x_ref, g_ref, w_ref, *rest):
    del cols_ref
    xn_ref = rest[-1]

    @pl.when(pl.program_id(1) == 0)
    def _():
        xn_ref[...] = _rms_scale(x_ref[...], g_ref[...]).astype(BF16)

    rest = rest[:-1]
    _inproj_body(xn_ref[...], w_ref, rest[-4] if len(rest) > 1 else rest[0],
                 rest[:3] + rest[-3:] if len(rest) > 1 else None)


def _inproj(x, w_in, src_cols, layer, tm, norm_g3=None, cast=None):
    n_tok = x.shape[0]
    n_j = NZ // TN_IN
    wspec = pl.BlockSpec((None, pl.Element(D_MODEL), pl.Element(TN_IN)),
                         lambda i, j, units: (layer, 0, units[j] * LANE))
    if norm_g3 is None:
        kern, scratch, operands = _inproj_kernel, [], [x, w_in]
        in_specs = [pl.BlockSpec((tm, D_MODEL), lambda i, j, units: (i, 0)), wspec]
    else:
        kern, scratch, operands = _inproj_norm_kernel, [pltpu.VMEM((tm, D_MODEL), BF16)], [x, norm_g3, w_in]
        in_specs = [pl.BlockSpec((tm, D_MODEL), lambda i, j, units: (i, 0), pipeline_mode=pl.Buffered(1)),
                    pl.BlockSpec((None, 1, D_MODEL), lambda i, j, units: (layer, 0, 0)), wspec]
    out_specs = [pl.BlockSpec((tm, TN_IN), lambda i, j, units: (i, j))]
    out_shape = [jax.ShapeDtypeStruct((n_tok, NZ), BF16)]
    if cast is not None:
        assert (n_tok // tm) * n_j >= N_CAST_O
        for w, n_blk in zip(cast, (N_CAST_A, N_CAST_A, N_CAST_O)):
            blk = lambda i, j, units, n_blk=n_blk: jnp.minimum(i * n_j + j, n_blk - 1)
            in_specs.append(pl.BlockSpec((None, CAST_ROWS, D_MODEL),
                                         lambda i, j, units, blk=blk: (layer, blk(i, j, units), 0)))
            out_specs.append(pl.BlockSpec((CAST_ROWS, D_MODEL), lambda i, j, units, blk=blk: (blk(i, j, units), 0)))
            out_shape.append(jax.ShapeDtypeStruct(w.shape[1:], BF16))
            operands.append(w)
    grid_spec = pltpu.PrefetchScalarGridSpec(
        num_scalar_prefetch=1,
        grid=(n_tok // tm, n_j),
        in_specs=in_specs,
        out_specs=out_specs,
        scratch_shapes=scratch,
    )
    return pl.pallas_call(
        kern,
        grid_spec=grid_spec,
        out_shape=out_shape,
        compiler_params=pltpu.CompilerParams(dimension_semantics=("arbitrary", "arbitrary"),
                                             vmem_limit_bytes=VMEM_LIMIT_PROJ),
        name="inproj",
    )(src_cols, *operands)


def _shift_mix(x, x_prev, mu):
    return x + (x_prev - x) * mu


def _lowrank_in(xwa):
    return jnp.where(_iota(xwa.shape, 1) < R_W, jnp.tanh(xwa), xwa).astype(BF16)


def _prep_gates(xr, xk, lr_w, lr_a, vecs):
    vec = lambda i: vecs[i:i + 1, :]
    lw = -EXP_M05 * jax.nn.sigmoid(vec(V_W0) + lr_w)
    ag = jax.nn.sigmoid(vec(V_A0) + lr_a)
    kk = xk * vec(V_KK)
    k2 = xk * (1.0 + (ag - 1.0) * vec(V_KA))
    return dict(lw=lw, ag=ag, kk=kk, k2=k2, seg_in=jnp.concatenate([kk * kk, xr * k2 * vec(V_RK)], axis=0))


def _prep_scale(xr, xv, gt, cl, seg, last_row):
    rows = cl.shape[0]
    kkn = gt["kk"] / jnp.maximum(jnp.sqrt(seg[:rows]), 1e-12)
    cl_last = last_row(cl)
    e_neg = jnp.exp(-cl)
    e_rem = jnp.exp(cl_last - cl)
    bb = kkn * gt["ag"]
    return dict(
        rt=xr * jnp.exp(cl),
        kt=gt["k2"] * e_neg,
        at=-kkn * jnp.exp(cl - gt["lw"]),
        bt=bb * e_neg,
        kd=gt["k2"] * e_rem,
        bd=bb * e_rem,
        v=xv,
        wl=jnp.exp(cl_last),
        bonus=seg[rows:] * xv,
    )


def _wkv_prep(r, k, v, wa, r_prev, k_prev, v_prev, wa_prev, vecs, w2a2, tri, pmat, last_row):
    vec = lambda i: vecs[i:i + 1, :]
    xr = _shift_mix(r, r_prev, vec(V_MU_R))
    xk = _shift_mix(k, k_prev, vec(V_MU_K))
    xv = _shift_mix(v, v_prev, vec(V_MU_V))
    lowrank = _dot(_lowrank_in(_shift_mix(wa, wa_prev, vecs[V_MU_WA:V_MU_WA + 1, :2 * R_W])), w2a2)
    gt = _prep_gates(xr, xk, lowrank[:, :D_R], lowrank[:, D_R:], vecs)
    return _prep_scale(xr, xv, gt, _cumsum_rows(tri, gt["lw"]), _segsum(gt["seg_in"], pmat), last_row)


def _wkv_finish(y, bonus, g_r, vecs, pmat):
    mu = _segsum(y, pmat) * (1.0 / HEAD)
    d = y - mu
    var = _segsum(d * d, pmat) * (1.0 / HEAD)
    yn = d * lax.rsqrt(var + GN_EPS) * vecs[V_LNG:V_LNG + 1, :] + vecs[V_LNB:V_LNB + 1, :]
    return ((yn + bonus) * (g_r * jax.nn.sigmoid(g_r))).astype(BF16)


def _grouped(x, nb):
    return x.reshape(nb, x.shape[0] // nb, x.shape[-1])


def _bcast_row(x, nb, j):
    x3 = _grouped(x, nb)
    return jnp.broadcast_to(x3[:, j:j + 1, :], x3.shape).reshape(x.shape)


def _prev_rows(x, prev, nb):
    t = x.shape[0] // nb
    first = (_iota(x.shape, 0) & (t - 1)) == 0
    return jnp.where(first, jnp.broadcast_to(prev, (nb, t, x.shape[-1])).reshape(x.shape), pltpu.roll(x, 1, 0))


def _interleave(major, minor):
    done = 0
    for i, piece in enumerate(major):
        piece()
        upto = (i + 1) * len(minor) // len(major)
        for other in minor[done:upto]:
            other()
        done = upto
    for other in minor[done:]:
        other()


def _chunk_buffers():
    L = WKV_L
    return [
        pltpu.VMEM((N_CHAIN, 2 * L, GL), BF16),
        pltpu.VMEM((N_CHAIN, L, GL), BF16),
        pltpu.VMEM((N_CHAIN, 2 * L, GL), F32),
        pltpu.VMEM((N_CHAIN, L, GL), BF16),
        pltpu.VMEM((N_CHAIN, 2 * L, GL), BF16),
        pltpu.VMEM((BATCH * L, D_R), F32),
        pltpu.VMEM((BATCH * L, D_R), F32),
        pltpu.VMEM((BATCH, 8, D_R), F32),
    ]


N_CHUNK_BUF = 8


def _wkv_prompt_kernel(r_ref, k_ref, v_ref, wa_ref, gr_ref, vecs_ref, w2a2_ref, tri_ref, pmat_ref, gmask_ref,
                       *rest):
    n_scr = 6 + 2 * N_CHUNK_BUF
    y_ref, s_out_ref, sh_out_ref = rest[-n_scr - 3:-n_scr]
    s_ref, pr_ref, pk_ref, pv_ref, pwa_ref, y_s = rest[-n_scr:-n_scr + 6]
    set_a = rest[-2 * N_CHUNK_BUF:-N_CHUNK_BUF]
    set_b = rest[-N_CHUNK_BUF:]
    prevs = (pr_ref, pk_ref, pv_ref, pwa_ref)
    s = pl.program_id(0)

    @pl.when(s == 0)
    def _():
        s_ref[...] = jnp.zeros_like(s_ref)
        for ref in prevs + tuple(set_b):
            ref[...] = jnp.zeros_like(ref)

    args = (r_ref, k_ref, v_ref, wa_ref, gr_ref, vecs_ref, w2a2_ref, tri_ref, pmat_ref, gmask_ref,
            y_ref, s_ref, prevs, y_s)

    @pl.when((s & 1) == 0)
    def _():
        _wkv_prompt_step(*args, set_a, set_b)

    @pl.when((s & 1) == 1)
    def _():
        _wkv_prompt_step(*args, set_b, set_a)

    @pl.when(s == pl.num_programs(0) - 1)
    def _():
        for b in range(BATCH):
            for g in range(N_HG):
                for j in range(HG):
                    s_out_ref[b, g * HG + j] = s_ref[b, g, j * HEAD:(j + 1) * HEAD, :][:, j * HEAD:(j + 1) * HEAD]
        sh_out_ref[:, :, 0:D_R] = pr_ref[...]
        sh_out_ref[:, :, D_R:2 * D_R] = pk_ref[...]
        sh_out_ref[:, :, 2 * D_R:3 * D_R] = pv_ref[...]
        sh_out_ref[:, :, 3 * D_R:D_SHIFT] = pwa_ref[...]


def _wkv_prompt_step(r_ref, k_ref, v_ref, wa_ref, gr_ref, vecs_ref, w2a2_ref, tri_ref, pmat_ref, gmask_ref,
                     y_ref, s_ref, prevs, y_s, wset, rset):
    L, nb = WKV_L, BATCH
    vecs = vecs_ref[...]
    pmat = pmat_ref[...]
    gmask = gmask_ref[...]
    gmask_f = gmask.astype(F32)
    tri = tri_ref[...]
    rs = lambda b: slice(b * L, (b + 1) * L)
    sl = lambda g: slice(g * GL, (g + 1) * GL)
    chains = [(b, g) for b in range(nb) for g in range(N_HG)]
    lhs_r, p_r, av_r, arb_r, zd_r, v_r, bonus_r, wl_r = rset
    lhs_w, p_w, av_w, arb_w, zd_w, v_w, bonus_w, wl_w = wset
    pr_ref, pk_ref, pv_ref, pwa_ref = prevs

    st = {}

    def b_state_terms(n, b, g):
        st["gs", n] = _mm_nt(lhs_r[n], s_ref[b, g])

    def b_u(n, b, g):
        st["u", n] = _dot(p_r[n], _block_diag(st["gs", n][:L] + av_r[n, 0:L, :], gmask))

    def b_y(n, b, g):
        y_s[rs(b), sl(g)] = st["gs", n][L:] + av_r[n, L:2 * L, :] + _dot(arb_r[n], _block_diag(st["u", n], gmask))

    def b_state(n, b, g):
        uv = jnp.concatenate([st["u", n], v_r[rs(b), sl(g)]], axis=0)
        s_ref[b, g] = (s_ref[b, g] * wl_r[b, 0:1, sl(g)] + _mm_tn(uv, zd_r[n])) * gmask_f

    def staged(fns, members):
        return [(lambda f=f, n=n, b=b, g=g: f(n, b, g)) for f in fns for n, (b, g) in members]

    state_part = staged([b_state_terms, b_u, b_y, b_state], list(enumerate(chains)))

    row = _iota((L, GL), 0)
    col = _iota((L, GL), 1) & (HEAD - 1)
    strict = row > col
    incl = row >= col
    row2 = _iota((2 * L, GL), 0)
    col2 = _iota((2 * L, GL), 1) & (HEAD - 1)
    both = (row2 - jnp.where(row2 < L, 0, L - 1)) > col2
    eye = jnp.where(row == col, 1.0, 0.0)
    pp = {}

    def prep_pieces(b):
        def mix_r():
            x = r_ref[b].astype(F32)
            pp["xr", b] = _shift_mix(x, _prev_rows(x, pr_ref[b:b + 1], 1), vecs[V_MU_R:V_MU_R + 1, :])
            pr_ref[b] = x[L - 1:L, :]

        def mix_k():
            x = k_ref[b].astype(F32)
            pp["xk", b] = _shift_mix(x, _prev_rows(x, pk_ref[b:b + 1], 1), vecs[V_MU_K:V_MU_K + 1, :])
            pk_ref[b] = x[L - 1:L, :]

        def mix_v():
            x = v_ref[b].astype(F32)
            xv = _shift_mix(x, _prev_rows(x, pv_ref[b:b + 1], 1), vecs[V_MU_V:V_MU_V + 1, :])
            pp["xv", b] = xv
            v_w[rs(b), :] = xv
            pv_ref[b] = x[L - 1:L, :]

        def lowrank():
            x = wa_ref[b].astype(F32)
            xwa = _shift_mix(x, _prev_rows(x, pwa_ref[b:b + 1], 1), vecs[V_MU_WA:V_MU_WA + 1, :2 * R_W])
            pwa_ref[b] = x[L - 1:L, :]
            pp["lr", b] = _dot(_lowrank_in(xwa), w2a2_ref[...])

        def gates(g):
            def run():
                xr, xk = pp["xr", b][:, sl(g)], pp["xk", b][:, sl(g)]
                lr = pp["lr", b]
                gt = _prep_gates(xr, xk, lr[:, sl(g)], lr[:, D_R + g * GL:D_R + (g + 1) * GL], vecs[:, sl(g)])
                pp["gt", b, g] = gt
                pp["cl", b, g] = _cumsum_rows(tri, gt["lw"])
                pp["seg", b, g] = _segsum(gt["seg_in"], pmat)
            return run

        def scale(g):
            def run():
                n = b * N_HG + g
                ops = _prep_scale(pp["xr", b][:, sl(g)], pp["xv", b][:, sl(g)], pp["gt", b, g], pp["cl", b, g],
                                  pp["seg", b, g], lambda x: jnp.broadcast_to(x[L - 1:L, :], x.shape))
                bonus_w[rs(b), sl(g)] = ops["bonus"]
                wl_w[b, :, sl(g)] = ops["wl"][0:8, :]
                lhs = jnp.concatenate([ops["at"], ops["rt"]], axis=0).astype(BF16)
                lhs_w[n] = lhs
                zd_w[n] = jnp.concatenate([ops["bd"], ops["kd"]], axis=0).astype(BF16)
                pp["lhs", n] = lhs
                pp["bk", n] = jnp.concatenate([_block_diag(ops["bt"], gmask), _block_diag(ops["kt"], gmask)],
                                              axis=0)
            return run

        return ([mix_r, mix_k, mix_v, lowrank] + [gates(g) for g in range(N_HG)]
                + [scale(g) for g in range(N_HG)])

    def a_scores(n, b, g):
        sc = _mm_nt(pp["lhs", n], pp["bk", n])
        a_ab = jnp.where(strict, sc[:L, :GL], 0.0)
        arb_w[n] = jnp.where(incl, sc[L:, :GL], 0.0).astype(BF16)
        pp["akv", n] = jnp.where(both, sc[:, GL:], 0.0).astype(BF16)
        pp["x", n] = a_ab.astype(BF16)
        pp["xbd", n] = _block_diag(a_ab, gmask)
        pp["p", n] = eye + a_ab

    def a_values(n, b, g):
        av_w[n] = _dot(pp["akv", n], _block_diag(pp["xv", b][:, sl(g)], gmask))

    def a_square(n, b, g):
        x = _dot(pp["x", n], pp["xbd", n]).astype(BF16)
        pp["x", n], pp["xbd", n] = x, _block_diag(x, gmask)

    def a_double(n, b, g):
        xp = _dot(jnp.concatenate([pp["x", n], pp["p", n].astype(BF16)], axis=0), pp["xbd", n])
        x = xp[:L].astype(BF16)
        pp["x", n], pp["xbd", n] = x, _block_diag(x, gmask)
        pp["p", n] = pp["p", n] + xp[L:]

    def a_inverse(n, b, g):
        p_w[n] = (pp["p", n] + _dot(pp["p", n].astype(BF16), pp["xbd", n])).astype(BF16)

    inv_stages = [a_scores, a_values, a_square] + [a_double] * (int(math.log2(L)) - 2) + [a_inverse]
    half = N_CHAIN // 2
    members = list(enumerate(chains))

    fin = {}

    def fin_mean(g):
        def run():
            y = y_s[:, sl(g)]
            fin[g] = y - _segsum(y, pmat) * (1.0 / HEAD)
        return run

    def fin_out(g):
        def run():
            d = fin[g]
            var = _segsum(d * d, pmat) * (1.0 / HEAD)
            yn = d * lax.rsqrt(var + GN_EPS) * vecs[V_LNG:V_LNG + 1, sl(g)] + vecs[V_LNB:V_LNB + 1, sl(g)]
            g_r = gr_ref[:, :, sl(g)].astype(F32).reshape(nb * L, GL)
            out = (yn + bonus_r[:, sl(g)]) * (g_r * jax.nn.sigmoid(g_r))
            y_ref[:, :, sl(g)] = out.astype(BF16).reshape(nb, L, GL)
        return run

    _interleave(state_part, prep_pieces(0) + prep_pieces(1))
    _interleave(staged(inv_stages, members[:half]), prep_pieces(2) + prep_pieces(3))
    _interleave(staged(inv_stages, members[half:]),
                [fin_mean(g) for g in range(N_HG)] + [fin_out(g) for g in range(N_HG)])


def _stack_io(bufs, n_in, first_out):
    return ([pl.BlockSpec(memory_space=pl.ANY)] * len(bufs), list(bufs),
            {n_in + i: first_out + i for i in range(len(bufs))})


def _wkv_prompt(z3, vecs, w2a2, tri, pmat, gmask, stacked, layer):
    L = WKV_L
    n_chunks = SEQ // L
    cur = lambda s: jnp.minimum(s, n_chunks - 1)
    prv = lambda s: jnp.maximum(s - 1, 0)
    zspec = lambda cb: pl.BlockSpec((BATCH, L, D_R), lambda s: (0, cur(s), cb))
    full = lambda a: pl.BlockSpec(a.shape, lambda s: (0,) * a.ndim)
    row_scr = lambda w: pltpu.VMEM((BATCH, 1, w), F32)
    extra_specs, extra_ops, aliases = _stack_io(stacked, 10, 1)
    return pl.pallas_call(
        _wkv_prompt_kernel,
        grid=(n_chunks + 1,),
        in_specs=[
            zspec(CB_R), zspec(CB_K), zspec(CB_V),
            pl.BlockSpec((BATCH, L, 2 * R_W), lambda s: (0, cur(s), CB_WA)),
            pl.BlockSpec((BATCH, L, D_R), lambda s: (0, prv(s), CB_GR)),
            pl.BlockSpec((None, N_VEC, D_R), lambda s: (layer, 0, 0)),
            pl.BlockSpec((None, 2 * R_W, 2 * D_R), lambda s: (layer, 0, 0)),
            full(tri), full(pmat), full(gmask),
        ] + extra_specs,
        out_specs=[
            pl.BlockSpec((BATCH, L, D_R), lambda s: (0, prv(s), 0)),
            pl.BlockSpec((None, BATCH, H_R, HEAD, HEAD), lambda s: (layer, 0, 0, 0, 0)),
            pl.BlockSpec((None, BATCH, 1, D_SHIFT), lambda s: (layer, 0, 0, 0)),
        ],
        out_shape=[
            jax.ShapeDtypeStruct((BATCH, SEQ, D_R), BF16),
            jax.ShapeDtypeStruct((DEPTH, BATCH, H_R, HEAD, HEAD), F32),
            jax.ShapeDtypeStruct((DEPTH, BATCH, 1, D_SHIFT), F32),
        ],
        scratch_shapes=[
            pltpu.VMEM((BATCH, N_HG, GL, GL), F32),
            row_scr(D_R), row_scr(D_R), row_scr(D_R), row_scr(2 * R_W),
            pltpu.VMEM((BATCH * L, D_R), F32),
        ] + _chunk_buffers() + _chunk_buffers(),
        input_output_aliases=aliases,
        compiler_params=_params("arbitrary"),
        name="wkv_prompt",
    )(z3, z3, z3, z3, z3, vecs, w2a2, tri, pmat, gmask, *extra_ops)


def _wkv_sample_kernel(r_ref, k_ref, v_ref, gr_ref, wa_ref, sh_in_ref, s_in_ref,
                       vecs_ref, w2a2_ref, tri_ref, pmat_ref, gmask_ref,
                       y_ref, s_out_ref, sh_out_ref, seg_s, u_s, zd_s, ga_s, gr_s):
    nb, T = SAMPLE_NB, DEC_SEQ
    R = nb * T
    vecs = vecs_ref[...]
    pmat = pmat_ref[...]
    gmask = gmask_ref[...]
    bc = lambda x, j: _bcast_row(x, nb, j)

    r, k, v, wa = [ref[...].astype(F32) for ref in (r_ref, k_ref, v_ref, wa_ref)]
    cols = ((0, D_R), (D_R, 2 * D_R), (2 * D_R, 3 * D_R), (3 * D_R, D_SHIFT))
    prev = [_prev_rows(x, sh_in_ref[:, :, lo:hi], nb) for x, (lo, hi) in zip((r, k, v, wa), cols)]
    for x, (lo, hi) in zip((r, k, v, wa), cols):
        sh_out_ref[:, :, lo:hi] = _grouped(x, nb)[:, T - 1:T, :]
    ops = _wkv_prep(r, k, v, wa, *prev, vecs, w2a2_ref[...], tri_ref[...], pmat, lambda x: bc(x, T - 1))
    rt, kt, at, bt, vv = ops["rt"], ops["kt"], ops["at"], ops["bt"], ops["v"]
    zd_s[...] = jnp.concatenate([ops["bd"], ops["kd"]], axis=0)

    sl = lambda g: slice(g * GL, (g + 1) * GL)
    groups = [(b, g) for b in range(nb) for g in range(N_HG)]

    t_idx = _iota((R, D_R), 0) & (T - 1)
    acc = {"u": jnp.zeros((R, D_R), F32), "y": jnp.zeros((R, D_R), F32)}

    def score_piece(j):
        def run():
            btj, ktj, vj = bc(bt, j), bc(kt, j), bc(vv, j)
            seg = _segsum(jnp.concatenate([at * btj, at * ktj, rt * btj, rt * ktj], axis=0), pmat)
            seg_s[j, 0] = seg[0:R]
            seg_s[j, 1] = seg[2 * R:3 * R]
            acc["u"] = acc["u"] + jnp.where(t_idx > j, seg[R:2 * R] * vj, 0.0)
            acc["y"] = acc["y"] + jnp.where(t_idx >= j, seg[3 * R:4 * R] * vj, 0.0)
        return run

    def state_term_piece(b, g):
        def run():
            s4 = s_in_ref[b, g * HG:(g + 1) * HG].reshape(GL, HEAD)
            s_bd = jnp.concatenate([s4.astype(BF16)] * HG, axis=1) * gmask
            lhs = jnp.concatenate([at[b * T:(b + 1) * T, sl(g)], rt[b * T:(b + 1) * T, sl(g)]], axis=0)
            gs = _mm_nt(lhs, s_bd)
            ga_s[b * T:(b + 1) * T, sl(g)] = gs[:T]
            gr_s[b * T:(b + 1) * T, sl(g)] = gs[T:]
        return run

    _interleave([state_term_piece(b, g) for b, g in groups], [score_piece(j) for j in range(T)])

    u = ga_s[...] + acc["u"]
    for j in range(T - 1):
        u = u + jnp.where(t_idx > j, seg_s[j, 0] * bc(u, j), 0.0)
    u_s[...] = jnp.concatenate([u, vv], axis=0)
    wl = ops["wl"]
    acc["y"] = acc["y"] + gr_s[...]

    def y_piece(j):
        def run():
            acc["y"] = acc["y"] + jnp.where(t_idx >= j, seg_s[j, 1] * bc(u, j), 0.0)
        return run

    def finish_piece():
        y_ref[...] = _wkv_finish(acc["y"], ops["bonus"], gr_ref[...].astype(F32), vecs, pmat)

    row2 = _iota((2 * R, GL), 0) & (R - 1)
    uv_t = {}

    def state_piece(g, b):
        def run():
            if b == 0:
                uv_t[g] = u_s[:, sl(g)].T.astype(BF16)
            mine = (row2 >= b * T) & (row2 < (b + 1) * T)
            upd = _dot(uv_t[g], jnp.where(mine, zd_s[:, sl(g)], 0.0).astype(BF16))
            for j in range(HG):
                h = g * HG + j
                s_out_ref[b, h] = (s_in_ref[b, h] * wl[b * T:b * T + 1, h * HEAD:(h + 1) * HEAD]
                                   + upd[j * HEAD:(j + 1) * HEAD, j * HEAD:(j + 1) * HEAD])
        return run

    _interleave([state_piece(g, b) for g in range(N_HG) for b in range(nb)],
                [y_piece(j) for j in range(T)] + [finish_piece])


def _wkv_sample(z, shift_buf, state_buf, vecs, w2a2, tri, pmat, gmask, layer):
    nb, T = SAMPLE_NB, DEC_SEQ
    R = nb * T
    zspec = lambda cb: pl.BlockSpec((R, D_R), lambda i: (i, cb))
    full = lambda a: pl.BlockSpec(a.shape, lambda i: (0,) * a.ndim)
    sspec = pl.BlockSpec((None, nb, H_R, HEAD, HEAD), lambda i: (layer, i, 0, 0, 0))
    shspec = pl.BlockSpec((None, nb, 1, D_SHIFT), lambda i: (layer, i, 0, 0))
    return pl.pallas_call(
        _wkv_sample_kernel,
        grid=(DEC_BATCH // nb,),
        in_specs=[
            zspec(CB_R), zspec(CB_K), zspec(CB_V), zspec(CB_GR),
            pl.BlockSpec((R, 2 * R_W), lambda i: (i, CB_WA)),
            shspec, sspec,
            pl.BlockSpec((None, N_VEC, D_R), lambda i: (layer, 0, 0)),
            pl.BlockSpec((None, 2 * R_W, 2 * D_R), lambda i: (layer, 0, 0)),
            full(tri), full(pmat), full(gmask),
        ],
        out_specs=[pl.BlockSpec((R, D_R), lambda i: (i, 0)), sspec, shspec],
        out_shape=[
            jax.ShapeDtypeStruct((N_SAMPLE, D_R), BF16),
            jax.ShapeDtypeStruct((DEPTH, DEC_BATCH, H_R, HEAD, HEAD), F32),
            jax.ShapeDtypeStruct((DEPTH, DEC_BATCH, 1, D_SHIFT), F32),
        ],
        scratch_shapes=[
            pltpu.VMEM((T, 2, R, D_R), F32),
            pltpu.VMEM((2 * R, D_R), F32),
            pltpu.VMEM((2 * R, D_R), F32),
            pltpu.VMEM((R, D_R), F32),
            pltpu.VMEM((R, D_R), F32),
        ],
        input_output_aliases={5: 2, 6: 1},
        compiler_params=_params("arbitrary"),
        name="wkv_sample",
    )(z, z, z, z, z, shift_buf, state_buf, vecs, w2a2, tri, pmat, gmask)


def _layer_norm(x, g, b):
    mu = jnp.mean(x, axis=-1, keepdims=True)
    d = x - mu
    var = jnp.mean(d * d, axis=-1, keepdims=True)
    return d * lax.rsqrt(var + LN_EPS) * g + b


def _sgu_sample_kernel(u_ref, vg_ref, gg_ref, lng_ref, lnb_ref, wc_ref, bias_ref, *rest):
    y_ref, vn_ref = rest[-2:]
    T = DEC_SEQ
    R = u_ref.shape[0]
    nb = R // T
    vn = _layer_norm(vg_ref[...].astype(F32), lng_ref[...], lnb_ref[...])
    vn_ref[...] = vn
    vn3 = vn.reshape(nb, T, D_G)
    s3 = jnp.broadcast_to(bias_ref[...][None], (nb, T, D_G))
    for j in range(T):
        s3 = s3 + jnp.broadcast_to(vn3[:, j:j + 1, :], vn3.shape) * wc_ref[j][None]
    gg = gg_ref[...].astype(F32)
    y_ref[...] = (u_ref[...].astype(F32) * s3.reshape(R, D_G) * (gg * jax.nn.sigmoid(gg))).astype(BF16)


def _sgu_sample(z, ln_g, ln_b, wc, bias_c, vn_buf, layer):
    R = 128
    zspec = lambda cb: pl.BlockSpec((R, D_G), lambda i: (i, cb))
    vspec = pl.BlockSpec((None, 1, D_G), lambda i: (layer, 0, 0))
    extra_specs, extra_ops, aliases = _stack_io((vn_buf,), 7, 1)
    return pl.pallas_call(
        _sgu_sample_kernel,
        grid=(N_SAMPLE // R,),
        in_specs=[
            zspec(CB_U), zspec(CB_VG), zspec(CB_GG), vspec, vspec,
            pl.BlockSpec((None, DEC_SEQ, DEC_SEQ, D_G), lambda i: (layer, 0, 0, 0)),
            pl.BlockSpec((None, DEC_SEQ, D_G), lambda i: (layer, 0, 0)),
        ] + extra_specs,
        out_specs=[pl.BlockSpec((R, D_G), lambda i: (i, 0)),
                   pl.BlockSpec((None, R, D_G), lambda i: (layer, i, 0))],
        out_shape=[jax.ShapeDtypeStruct((N_SAMPLE, D_G), BF16),
                   jax.ShapeDtypeStruct((DEPTH, N_SAMPLE, D_G), F32)],
        input_output_aliases=aliases,
        compiler_params=_params("arbitrary"),
        name="sgu_sample",
    )(z, z, z, ln_g, ln_b, wc, bias_c, *extra_ops)


def _outproj_tail(ya, yb_in, ga_ref, gb_ref, h_ref, wb_ref, wo_ref, gn_ref, outs):
    yb = _dot(yb_in, wb_ref[...])
    m = jax.nn.sigmoid(ga_ref[...].astype(F32)) * ya + jax.nn.sigmoid(gb_ref[...].astype(F32)) * yb
    h_new = h_ref[...] + _dot(m.astype(BF16), wo_ref[...])
    xn = _rms_scale(h_new, gn_ref[...])
    if len(outs) == 1:
        outs[0][...] = xn
    else:
        outs[0][...] = h_new
        outs[1][...] = xn.astype(BF16)


def _outproj_kernel(ya_ref, yb_ref, ga_ref, gb_ref, h_ref, wa_ref, wb_ref, wo_ref, gn_ref, *outs):
    _outproj_tail(_dot(ya_ref[...], wa_ref[...]), yb_ref[...], ga_ref, gb_ref, h_ref, wb_ref, wo_ref, gn_ref, outs)


def _outproj_sgu_kernel(ya_ref, u_ref, vg_ref, gg_ref, lng_ref, lnb_ref, sw_ref, sbias_ref,
                        ga_ref, gb_ref, h_ref, wa_ref, wb_ref, wo_ref, gn_ref, *outs):
    L = SGU_L
    n_col = 8
    cw = D_MODEL // n_col
    st = {}

    def ya_piece(c):
        def run():
            st["ya", c] = _dot(ya_ref[...], wa_ref[:, c * cw:(c + 1) * cw])
        return run

    causal = _iota((L, L), 0) >= _iota((L, L), 1)

    def sgu_norm(c):
        def run():
            st["vn", c] = _layer_norm(vg_ref[c * L:(c + 1) * L, :].astype(F32), lng_ref[...],
                                      lnb_ref[...]).astype(BF16)
        return run

    def sgu_mix(c):
        def run():
            parts = [_dot(jnp.where(causal, sw_ref[g], 0.0).astype(BF16), st["vn", c][:, g * GW:(g + 1) * GW])
                     for g in range(N_GROUPS)]
            gg = gg_ref[c * L:(c + 1) * L, :].astype(F32)
            st["yb", c] = (u_ref[c * L:(c + 1) * L, :].astype(F32) * (jnp.concatenate(parts, axis=1) + sbias_ref[...])
                           * (gg * jax.nn.sigmoid(gg))).astype(BF16)
        return run

    n_chunks = ya_ref.shape[0] // L
    _interleave([ya_piece(c) for c in range(n_col)],
                [f(c) for c in range(n_chunks) for f in (sgu_norm, sgu_mix)])
    ya = jnp.concatenate([st["ya", c] for c in range(n_col)], axis=1)
    yb_in = jnp.concatenate([st["yb", c] for c in range(n_chunks)], axis=0)
    _outproj_tail(ya, yb_in, ga_ref, gb_ref, h_ref, wb_ref, wo_ref, gn_ref, outs)


def _outproj(ya_in, yb_in, z, h, wpa, wpb, wout, gn, layer, last, sgu=None):
    n_tok = h.shape[0]
    tm = TM_OUT
    once = lambda shape, imap: pl.BlockSpec(shape, imap, pipeline_mode=pl.Buffered(1))
    wspec = lambda a: once(a.shape, lambda i: (0, 0))
    row = lambda w: pl.BlockSpec((tm, w), lambda i: (i, 0))
    zcol = lambda w, cb: pl.BlockSpec((tm, w), lambda i: (i, cb))
    tok = lambda dt: jax.ShapeDtypeStruct((n_tok, D_MODEL), dt)
    vspec = pl.BlockSpec((None, 1, D_G), lambda i: (layer, 0, 0))
    if sgu is None:
        kern, branch_specs, branch_ops = _outproj_kernel, [row(D_G)], (yb_in,)
    else:
        kern = _outproj_sgu_kernel
        branch_specs = [zcol(D_G, CB_U), zcol(D_G, CB_VG), zcol(D_G, CB_GG), vspec, vspec,
                        once((None, N_GROUPS, SGU_L, SGU_L), lambda i: (layer, 0, 0, 0)),
                        once((None, SGU_L, D_G), lambda i: (layer, 0, 0))]
        branch_ops = (z, z, z) + tuple(sgu)
    return pl.pallas_call(
        kern,
        grid=(n_tok // tm,),
        in_specs=[row(D_R)] + branch_specs + [
            zcol(D_MODEL, 0), zcol(D_MODEL, 1), row(D_MODEL),
            wspec(wpa), wspec(wpb), wspec(wout),
            pl.BlockSpec((None, 1, D_MODEL), lambda i: (gn[1], 0, 0)),
        ],
        out_specs=[row(D_MODEL)] if last else [row(D_MODEL), row(D_MODEL)],
        out_shape=[tok(F32)] if last else [tok(F32), tok(BF16)],
        compiler_params=pltpu.CompilerParams(dimension_semantics=("arbitrary",), vmem_limit_bytes=VMEM_LIMIT_PROJ),
        name="outproj",
    )(ya_in, *branch_ops, z, z, h, wpa, wpb, wout, gn[0])


def _block_tri(n, blk):
    i = jnp.arange(n)
    return ((i[:, None] >= i[None, :]) & (i[:, None] // blk == i[None, :] // blk)).astype(BF16)


def kernel(x_prompt, x_sample, state_wkv, state_shift, norm_g, w_in, shift_mu, w0, w2, a0, a2, k_k, k_a, r_k,
           lnx_g, lnx_b, sgu_ln_g, sgu_ln_b, sgu_w, sgu_b, w_proj_a, w_proj_b, w_out, final_norm_g):
    c_wd, c_gr = 3 * D_R, D_SHIFT
    proj_w = (w_proj_a, w_proj_b, w_out)

    zrow = jnp.zeros((DEPTH, D_R), F32)
    mu_wa = jnp.pad(shift_mu[:, c_wd:c_gr], ((0, 0), (0, D_R - 2 * R_W)))
    vec_rows = [shift_mu[:, 0:D_R], shift_mu[:, D_R:2 * D_R], shift_mu[:, 2 * D_R:3 * D_R], w0, a0, k_k, k_a,
                r_k.reshape(DEPTH, D_R), lnx_g, lnx_b, mu_wa] + [zrow] * (N_VEC - 11)
    vecs = jnp.stack(vec_rows, axis=1)
    zblk = jnp.zeros((DEPTH, R_W, D_R), F32)
    w2a2 = jnp.concatenate([jnp.concatenate([w2, zblk], axis=2),
                            jnp.concatenate([zblk, a2], axis=2)], axis=1).astype(BF16)
    norm_g3 = norm_g.reshape(DEPTH, 1, D_MODEL)
    final_g3 = final_norm_g.reshape(1, 1, D_MODEL)
    ln_g3, ln_b3 = sgu_ln_g.reshape(DEPTH, 1, D_G), sgu_ln_b.reshape(DEPTH, 1, D_G)
    bias_exp = jnp.repeat(jnp.swapaxes(sgu_b, 1, 2), GW, axis=2)
    tmask = jnp.tril(jnp.ones((DEC_SEQ, DEC_SEQ), F32))
    wc = jnp.repeat(jnp.transpose(sgu_w[:, :, :DEC_SEQ, :DEC_SEQ] * tmask, (0, 3, 2, 1)), GW, axis=3)
    bias_c = bias_exp[:, :DEC_SEQ, :]

    pmat = ((jnp.arange(GL)[:, None] // HEAD) == (jnp.arange(GL)[None, :] // HEAD)).astype(BF16)
    tri_p = _block_tri(WKV_L, WKV_L)
    tri_s = _block_tri(SAMPLE_NB * DEC_SEQ, DEC_SEQ)
    src_cols = jnp.asarray(Z_SRC_UNITS, jnp.int32)

    h_p = x_prompt.reshape(N_PROMPT, D_MODEL)
    h_s = x_sample.reshape(N_SAMPLE, D_MODEL)
    xn_p = xn_s = None
    stk_p = [jnp.zeros((DEPTH, BATCH, H_R, HEAD, HEAD), F32), jnp.zeros((DEPTH, BATCH, 1, D_SHIFT), F32)]
    wkv_s, shift_s = state_wkv, state_shift
    vn_s = jnp.zeros((DEPTH, N_SAMPLE, D_G), F32)
    for l in range(DEPTH):
        last = l == DEPTH - 1
        if l == 0:
            z_p, wpa, wpb, wout = _inproj(h_p, w_in, src_cols, l, TM_IN_NORM, norm_g3, cast=proj_w)
            z_s, = _inproj(h_s, w_in, src_cols, l, TM_IN_SAMPLE, norm_g3)
        else:
            z_p, wpa, wpb, wout = _inproj(xn_p, w_in, src_cols, l, TM_IN_PROMPT, cast=proj_w)
            z_s, = _inproj(xn_s, w_in, src_cols, l, TM_IN_SAMPLE)
        ya_p, *stk_p = _wkv_prompt(z_p.reshape(BATCH, SEQ, NZ), vecs, w2a2, tri_p, pmat, pmat, stk_p, l)
        ya_s, wkv_s, shift_s = _wkv_sample(z_s, shift_s, wkv_s, vecs, w2a2, tri_s, pmat, pmat, l)
        yb_s, vn_s = _sgu_sample(z_s, ln_g3, ln_b3, wc, bias_c, vn_s, l)
        gn = (final_g3, 0) if last else (norm_g3, l + 1)
        out_p = _outproj(ya_p.reshape(N_PROMPT, D_R), None, z_p, h_p, wpa, wpb, wout, gn, l, last,
                         sgu=(ln_g3, ln_b3, sgu_w, bias_exp))
        out_s = _outproj(ya_s, yb_s, z_s, h_s, wpa, wpb, wout, gn, l, last)
        if last:
            y_p, y_s = out_p[0], out_s[0]
        else:
            (h_p, xn_p), (h_s, xn_s) = out_p, out_s
    return (y_p.reshape(BATCH, SEQ, D_MODEL), y_s.reshape(DEC_BATCH, DEC_SEQ, D_MODEL),
            stk_p[0], stk_p[1], wkv_s, shift_s, vn_s.reshape(DEPTH, DEC_BATCH, DEC_SEQ, D_G))
```

```python
import math

import jax
import jax.numpy as jnp
from jax import lax
from jax.experimental import pallas as pl
from jax.experimental.pallas import tpu as pltpu

F32 = jnp.float32
BF16 = jnp.bfloat16

D_MODEL = 2048
BATCH = 4
SEQ = 2048
DEPTH = 4
DEC_BATCH = 128
DEC_SEQ = 8
HEAD = 64
D_R = 1024
H_R = 16
R_W = 64
R_A = 64
D_G = 1024
N_GROUPS = 8
GW = 128
D_SHIFT = 3 * D_R + R_W + R_A
RMS_EPS = 1e-6
LN_EPS = 1e-5
GN_EPS = 64e-5
EXP_M05 = math.exp(-0.5)

N_PROMPT = BATCH * SEQ
N_SAMPLE = DEC_BATCH * DEC_SEQ

LANE = 128
CB_U, CB_VG, CB_GG, CB_R, CB_K, CB_V, CB_GR = 4, 5, 6, 7, 8, 9, 10
CB_WA = 88
TN_IN = 512
NZ = 23 * TN_IN
Z_SRC_UNITS = tuple([57 + 4 * j for j in range(8)] + [33 + 4 * j for j in range(6)] + [4 * j for j in range(6)]
                    + [25, 29] + [24])

HG = 4
GL = HG * HEAD
N_HG = H_R // HG

WKV_L = 64
N_CHAIN = BATCH * N_HG
SAMPLE_NB = 8

TM_IN_PROMPT, TM_IN_SAMPLE = 4096, 1024
TM_IN_NORM = 2048
TM_OUT = 512
SGU_L = 128
VMEM_LIMIT = 56 * 1024 * 1024
VMEM_LIMIT_PROJ = 62 * 1024 * 1024

V_MU_R, V_MU_K, V_MU_V, V_W0, V_A0, V_KK, V_KA, V_RK, V_LNG, V_LNB, V_MU_WA = range(11)
N_VEC = 16


def _dot(a, b):
    return jnp.dot(a, b, preferred_element_type=F32)


def _mm_nt(a, b):
    return lax.dot_general(a.astype(BF16), b.astype(BF16), (((1,), (1,)), ((), ())),
                           preferred_element_type=F32)


def _mm_tn(a, b):
    return lax.dot_general(a.astype(BF16), b.astype(BF16), (((0,), (0,)), ((), ())),
                           preferred_element_type=F32)


def _iota(shape, dim):
    return lax.broadcasted_iota(jnp.int32, shape, dim)


def _block_diag(x, gmask_bf):
    reps = GL // x.shape[0]
    return jnp.concatenate([x.astype(BF16)] * reps, axis=0) * gmask_bf


def _segsum(x, pmat):
    rows, n = x.shape[0], x.shape[1] // GL
    xb = x.astype(BF16)
    stacked = jnp.concatenate([xb[:, g * GL:(g + 1) * GL] for g in range(n)], axis=0)
    s = _dot(stacked, pmat)
    return jnp.concatenate([s[g * rows:(g + 1) * rows] for g in range(n)], axis=1)


def _cumsum_rows(tri, x):
    hi = x.astype(BF16)
    lo = (x - hi.astype(F32)).astype(BF16)
    return _dot(tri, hi) + _dot(tri, lo)


def _rms_scale(x, g):
    ms = jnp.mean(x * x, axis=-1, keepdims=True)
    return x * lax.rsqrt(ms + RMS_EPS) * g


def _params(*sem):
    return pltpu.CompilerParams(dimension_semantics=sem, vmem_limit_bytes=VMEM_LIMIT)


CAST_ROWS = 64
N_CAST_A = D_R // CAST_ROWS
N_CAST_O = D_MODEL // CAST_ROWS


def _inproj_body(x_bf, w_ref, z_ref, casts):
    z_ref[...] = _dot(x_bf, w_ref[...].astype(BF16)).astype(z_ref.dtype)
    if casts:
        wa_ref, wb_ref, wo_ref, wa_out, wb_out, wo_out = casts
        step = pl.program_id(0) * pl.num_programs(1) + pl.program_id(1)

        @pl.when(step < N_CAST_A)
        def _():
            wa_out[...] = wa_ref[...].astype(BF16)
            wb_out[...] = wb_ref[...].astype(BF16)

        @pl.when(step < N_CAST_O)
        def _():
            wo_out[...] = wo_ref[...].astype(BF16)


def _inproj_kernel(cols_ref, x_ref, w_ref, *rest):
    del cols_ref
    _inproj_body(x_ref[...], w_ref, rest[-4] if len(rest) > 1 else rest[0],
                 rest[:3] + rest[-3:] if len(rest) > 1 else None)


def _inproj_norm_kernel(cols_ref, x_ref, g_ref, w_ref, *rest):
    del cols_ref
    xn_ref = rest[-1]

    @pl.when(pl.program_id(1) == 0)
    def _():
        xn_ref[...] = _rms_scale(x_ref[...], g_ref[...]).astype(BF16)

    rest = rest[:-1]
    _inproj_body(xn_ref[...], w_ref, rest[-4] if len(rest) > 1 else rest[0],
                 rest[:3] + rest[-3:] if len(rest) > 1 else None)


def _inproj(x, w_in, src_cols, layer, tm, norm_g3=None, cast=None):
    n_tok = x.shape[0]
    n_j = NZ // TN_IN
    wspec = pl.BlockSpec((None, pl.Element(D_MODEL), pl.Element(TN_IN)),
                         lambda i, j, units: (layer, 0, units[j] * LANE))
    if norm_g3 is None:
        kern, scratch, operands = _inproj_kernel, [], [x, w_in]
        in_specs = [pl.BlockSpec((tm, D_MODEL), lambda i, j, units: (i, 0)), wspec]
    else:
        kern, scratch, operands = _inproj_norm_kernel, [pltpu.VMEM((tm, D_MODEL), BF16)], [x, norm_g3, w_in]
        in_specs = [pl.BlockSpec((tm, D_MODEL), lambda i, j, units: (i, 0)),
                    pl.BlockSpec((None, 1, D_MODEL), lambda i, j, units: (layer, 0, 0)), wspec]
    out_specs = [pl.BlockSpec((tm, TN_IN), lambda i, j, units: (i, j))]
    out_shape = [jax.ShapeDtypeStruct((n_tok, NZ), BF16)]
    if cast is not None:
        assert (n_tok // tm) * n_j >= N_CAST_O
        for w, n_blk in zip(cast, (N_CAST_A, N_CAST_A, N_CAST_O)):
            blk = lambda i, j, units, n_blk=n_blk: jnp.minimum(i * n_j + j, n_blk - 1)
            in_specs.append(pl.BlockSpec((None, CAST_ROWS, D_MODEL),
                                         lambda i, j, units, blk=blk: (layer, blk(i, j, units), 0)))
            out_specs.append(pl.BlockSpec((CAST_ROWS, D_MODEL), lambda i, j, units, blk=blk: (blk(i, j, units), 0)))
            out_shape.append(jax.ShapeDtypeStruct(w.shape[1:], BF16))
            operands.append(w)
    grid_spec = pltpu.PrefetchScalarGridSpec(
        num_scalar_prefetch=1,
        grid=(n_tok // tm, n_j),
        in_specs=in_specs,
        out_specs=out_specs,
        scratch_shapes=scratch,
    )
    return pl.pallas_call(
        kern,
        grid_spec=grid_spec,
        out_shape=out_shape,
        compiler_params=pltpu.CompilerParams(dimension_semantics=("arbitrary", "arbitrary"),
                                             vmem_limit_bytes=VMEM_LIMIT_PROJ),
        name="inproj",
    )(src_cols, *operands)


def _shift_mix(x, x_prev, mu):
    return x + (x_prev - x) * mu


def _lowrank_in(xwa):
    return jnp.where(_iota(xwa.shape, 1) < R_W, jnp.tanh(xwa), xwa).astype(BF16)


def _prep_gates(xr, xk, lr_w, lr_a, vecs):
    vec = lambda i: vecs[i:i + 1, :]
    lw = -EXP_M05 * jax.nn.sigmoid(vec(V_W0) + lr_w)
    ag = jax.nn.sigmoid(vec(V_A0) + lr_a)
    kk = xk * vec(V_KK)
    k2 = xk * (1.0 + (ag - 1.0) * vec(V_KA))
    return dict(lw=lw, ag=ag, kk=kk, k2=k2, seg_in=jnp.concatenate([kk * kk, xr * k2 * vec(V_RK)], axis=0))


def _prep_scale(xr, xv, gt, cl, seg, last_row):
    rows = cl.shape[0]
    kkn = gt["kk"] / jnp.maximum(jnp.sqrt(seg[:rows]), 1e-12)
    cl_last = last_row(cl)
    e_neg = jnp.exp(-cl)
    e_rem = jnp.exp(cl_last - cl)
    bb = kkn * gt["ag"]
    return dict(
        rt=xr * jnp.exp(cl),
        kt=gt["k2"] * e_neg,
        at=-kkn * jnp.exp(cl - gt["lw"]),
        bt=bb * e_neg,
        kd=gt["k2"] * e_rem,
        bd=bb * e_rem,
        v=xv,
        wl=jnp.exp(cl_last),
        bonus=seg[rows:] * xv,
    )


def _wkv_prep(r, k, v, wa, r_prev, k_prev, v_prev, wa_prev, vecs, w2a2, tri, pmat, last_row):
    vec = lambda i: vecs[i:i + 1, :]
    xr = _shift_mix(r, r_prev, vec(V_MU_R))
    xk = _shift_mix(k, k_prev, vec(V_MU_K))
    xv = _shift_mix(v, v_prev, vec(V_MU_V))
    lowrank = _dot(_lowrank_in(_shift_mix(wa, wa_prev, vecs[V_MU_WA:V_MU_WA + 1, :2 * R_W])), w2a2)
    gt = _prep_gates(xr, xk, lowrank[:, :D_R], lowrank[:, D_R:], vecs)
    return _prep_scale(xr, xv, gt, _cumsum_rows(tri, gt["lw"]), _segsum(gt["seg_in"], pmat), last_row)


def _wkv_finish(y, bonus, g_r, vecs, pmat):
    mu = _segsum(y, pmat) * (1.0 / HEAD)
    d = y - mu
    var = _segsum(d * d, pmat) * (1.0 / HEAD)
    yn = d * lax.rsqrt(var + GN_EPS) * vecs[V_LNG:V_LNG + 1, :] + vecs[V_LNB:V_LNB + 1, :]
    return ((yn + bonus) * (g_r * jax.nn.sigmoid(g_r))).astype(BF16)


def _grouped(x, nb):
    return x.reshape(nb, x.shape[0] // nb, x.shape[-1])


def _bcast_row(x, nb, j):
    x3 = _grouped(x, nb)
    return jnp.broadcast_to(x3[:, j:j + 1, :], x3.shape).reshape(x.shape)


def _prev_rows(x, prev, nb):
    t = x.shape[0] // nb
    first = (_iota(x.shape, 0) & (t - 1)) == 0
    return jnp.where(first, jnp.broadcast_to(prev, (nb, t, x.shape[-1])).reshape(x.shape), pltpu.roll(x, 1, 0))


def _interleave(major, minor):
    done = 0
    for i, piece in enumerate(major):
        piece()
        upto = (i + 1) * len(minor) // len(major)
        for other in minor[done:upto]:
            other()
        done = upto
    for other in minor[done:]:
        other()


def _chunk_buffers():
    L = WKV_L
    return [
        pltpu.VMEM((N_CHAIN, 2 * L, GL), BF16),
        pltpu.VMEM((N_CHAIN, L, GL), BF16),
        pltpu.VMEM((N_CHAIN, 2 * L, GL), F32),
        pltpu.VMEM((N_CHAIN, L, GL), BF16),
        pltpu.VMEM((N_CHAIN, 2 * L, GL), BF16),
        pltpu.VMEM((BATCH * L, D_R), F32),
        pltpu.VMEM((BATCH * L, D_R), F32),
        pltpu.VMEM((BATCH, 8, D_R), F32),
    ]


N_CHUNK_BUF = 8


def _wkv_prompt_kernel(r_ref, k_ref, v_ref, wa_ref, gr_ref, vecs_ref, w2a2_ref, tri_ref, pmat_ref, gmask_ref,
                       *rest):
    n_scr = 6 + 2 * N_CHUNK_BUF
    y_ref, s_out_ref, sh_out_ref = rest[-n_scr - 3:-n_scr]
    s_ref, pr_ref, pk_ref, pv_ref, pwa_ref, y_s = rest[-n_scr:-n_scr + 6]
    set_a = rest[-2 * N_CHUNK_BUF:-N_CHUNK_BUF]
    set_b = rest[-N_CHUNK_BUF:]
    prevs = (pr_ref, pk_ref, pv_ref, pwa_ref)
    s = pl.program_id(0)

    @pl.when(s == 0)
    def _():
        s_ref[...] = jnp.zeros_like(s_ref)
        for ref in prevs + tuple(set_b):
            ref[...] = jnp.zeros_like(ref)

    args = (r_ref, k_ref, v_ref, wa_ref, gr_ref, vecs_ref, w2a2_ref, tri_ref, pmat_ref, gmask_ref,
            y_ref, s_ref, prevs, y_s)

    @pl.when((s & 1) == 0)
    def _():
        _wkv_prompt_step(*args, set_a, set_b)

    @pl.when((s & 1) == 1)
    def _():
        _wkv_prompt_step(*args, set_b, set_a)

    @pl.when(s == pl.num_programs(0) - 1)
    def _():
        for b in range(BATCH):
            for g in range(N_HG):
                for j in range(HG):
                    s_out_ref[b, g * HG + j] = s_ref[b, g, j * HEAD:(j + 1) * HEAD, :][:, j * HEAD:(j + 1) * HEAD]
        sh_out_ref[:, :, 0:D_R] = pr_ref[...]
        sh_out_ref[:, :, D_R:2 * D_R] = pk_ref[...]
        sh_out_ref[:, :, 2 * D_R:3 * D_R] = pv_ref[...]
        sh_out_ref[:, :, 3 * D_R:D_SHIFT] = pwa_ref[...]


def _wkv_prompt_step(r_ref, k_ref, v_ref, wa_ref, gr_ref, vecs_ref, w2a2_ref, tri_ref, pmat_ref, gmask_ref,
                     y_ref, s_ref, prevs, y_s, wset, rset):
    L, nb = WKV_L, BATCH
    vecs = vecs_ref[...]
    pmat = pmat_ref[...]
    gmask = gmask_ref[...]
    gmask_f = gmask.astype(F32)
    tri = tri_ref[...]
    rs = lambda b: slice(b * L, (b + 1) * L)
    sl = lambda g: slice(g * GL, (g + 1) * GL)
    chains = [(b, g) for b in range(nb) for g in range(N_HG)]
    lhs_r, p_r, av_r, arb_r, zd_r, v_r, bonus_r, wl_r = rset
    lhs_w, p_w, av_w, arb_w, zd_w, v_w, bonus_w, wl_w = wset
    pr_ref, pk_ref, pv_ref, pwa_ref = prevs

    st = {}

    def b_state_terms(n, b, g):
        st["gs", n] = _mm_nt(lhs_r[n], s_ref[b, g])

    def b_u(n, b, g):
        st["u", n] = _dot(p_r[n], _block_diag(st["gs", n][:L] + av_r[n, 0:L, :], gmask))

    def b_y(n, b, g):
        y_s[rs(b), sl(g)] = st["gs", n][L:] + av_r[n, L:2 * L, :] + _dot(arb_r[n], _block_diag(st["u", n], gmask))

    def b_state(n, b, g):
        uv = jnp.concatenate([st["u", n], v_r[rs(b), sl(g)]], axis=0)
        s_ref[b, g] = (s_ref[b, g] * wl_r[b, 0:1, sl(g)] + _mm_tn(uv, zd_r[n])) * gmask_f

    def staged(fns, members):
        return [(lambda f=f, n=n, b=b, g=g: f(n, b, g)) for f in fns for n, (b, g) in members]

    state_part = staged([b_state_terms, b_u, b_y, b_state], list(enumerate(chains)))

    row = _iota((L, GL), 0)
    col = _iota((L, GL), 1) & (HEAD - 1)
    strict = row > col
    incl = row >= col
    row2 = _iota((2 * L, GL), 0)
    col2 = _iota((2 * L, GL), 1) & (HEAD - 1)
    both = (row2 - jnp.where(row2 < L, 0, L - 1)) > col2
    eye = jnp.where(row == col, 1.0, 0.0)
    pp = {}

    def prep_pieces(b):
        def mix_r():
            x = r_ref[b].astype(F32)
            pp["xr", b] = _shift_mix(x, _prev_rows(x, pr_ref[b:b + 1], 1), vecs[V_MU_R:V_MU_R + 1, :])
            pr_ref[b] = x[L - 1:L, :]

        def mix_k():
            x = k_ref[b].astype(F32)
            pp["xk", b] = _shift_mix(x, _prev_rows(x, pk_ref[b:b + 1], 1), vecs[V_MU_K:V_MU_K + 1, :])
            pk_ref[b] = x[L - 1:L, :]

        def mix_v():
            x = v_ref[b].astype(F32)
            xv = _shift_mix(x, _prev_rows(x, pv_ref[b:b + 1], 1), vecs[V_MU_V:V_MU_V + 1, :])
            pp["xv", b] = xv
            v_w[rs(b), :] = xv
            pv_ref[b] = x[L - 1:L, :]

        def lowrank():
            x = wa_ref[b].astype(F32)
            xwa = _shift_mix(x, _prev_rows(x, pwa_ref[b:b + 1], 1), vecs[V_MU_WA:V_MU_WA + 1, :2 * R_W])
            pwa_ref[b] = x[L - 1:L, :]
            pp["lr", b] = _dot(_lowrank_in(xwa), w2a2_ref[...])

        def gates(g):
            def run():
                xr, xk = pp["xr", b][:, sl(g)], pp["xk", b][:, sl(g)]
                lr = pp["lr", b]
                gt = _prep_gates(xr, xk, lr[:, sl(g)], lr[:, D_R + g * GL:D_R + (g + 1) * GL], vecs[:, sl(g)])
                pp["gt", b, g] = gt
                pp["cl", b, g] = _cumsum_rows(tri, gt["lw"])
                pp["seg", b, g] = _segsum(gt["seg_in"], pmat)
            return run

        def scale(g):
            def run():
                n = b * N_HG + g
                ops = _prep_scale(pp["xr", b][:, sl(g)], pp["xv", b][:, sl(g)], pp["gt", b, g], pp["cl", b, g],
                                  pp["seg", b, g], lambda x: jnp.broadcast_to(x[L - 1:L, :], x.shape))
                bonus_w[rs(b), sl(g)] = ops["bonus"]
                wl_w[b, :, sl(g)] = ops["wl"][0:8, :]
                lhs = jnp.concatenate([ops["at"], ops["rt"]], axis=0).astype(BF16)
                lhs_w[n] = lhs
                zd_w[n] = jnp.concatenate([ops["bd"], ops["kd"]], axis=0).astype(BF16)
                pp["lhs", n] = lhs
                pp["bk", n] = jnp.concatenate([_block_diag(ops["bt"], gmask), _block_diag(ops["kt"], gmask)],
                                              axis=0)
            return run

        return ([mix_r, mix_k, mix_v, lowrank] + [gates(g) for g in range(N_HG)]
                + [scale(g) for g in range(N_HG)])

    def a_scores(n, b, g):
        sc = _mm_nt(pp["lhs", n], pp["bk", n])
        a_ab = jnp.where(strict, sc[:L, :GL], 0.0)
        arb_w[n] = jnp.where(incl, sc[L:, :GL], 0.0).astype(BF16)
        pp["akv", n] = jnp.where(both, sc[:, GL:], 0.0).astype(BF16)
        pp["x", n] = a_ab.astype(BF16)
        pp["xbd", n] = _block_diag(a_ab, gmask)
        pp["p", n] = eye + a_ab

    def a_values(n, b, g):
        av_w[n] = _dot(pp["akv", n], _block_diag(pp["xv", b][:, sl(g)], gmask))

    def a_square(n, b, g):
        x = _dot(pp["x", n], pp["xbd", n]).astype(BF16)
        pp["x", n], pp["xbd", n] = x, _block_diag(x, gmask)

    def a_double(n, b, g):
        xp = _dot(jnp.concatenate([pp["x", n], pp["p", n].astype(BF16)], axis=0), pp["xbd", n])
        x = xp[:L].astype(BF16)
        pp["x", n], pp["xbd", n] = x, _block_diag(x, gmask)
        pp["p", n] = pp["p", n] + xp[L:]

    def a_inverse(n, b, g):
        p_w[n] = (pp["p", n] + _dot(pp["p", n].astype(BF16), pp["xbd", n])).astype(BF16)

    inv_stages = [a_scores, a_values, a_square] + [a_double] * (int(math.log2(L)) - 2) + [a_inverse]
    half = N_CHAIN // 2
    members = list(enumerate(chains))

    fin = {}

    def fin_mean(g):
        def run():
            y = y_s[:, sl(g)]
            fin[g] = y - _segsum(y, pmat) * (1.0 / HEAD)
        return run

    def fin_out(g):
        def run():
            d = fin[g]
            var = _segsum(d * d, pmat) * (1.0 / HEAD)
            yn = d * lax.rsqrt(var + GN_EPS) * vecs[V_LNG:V_LNG + 1, sl(g)] + vecs[V_LNB:V_LNB + 1, sl(g)]
            g_r = gr_ref[:, :, sl(g)].astype(F32).reshape(nb * L, GL)
            out = (yn + bonus_r[:, sl(g)]) * (g_r * jax.nn.sigmoid(g_r))
            y_ref[:, :, sl(g)] = out.astype(BF16).reshape(nb, L, GL)
        return run

    _interleave(state_part, prep_pieces(0) + prep_pieces(1))
    _interleave(staged(inv_stages, members[:half]), prep_pieces(2) + prep_pieces(3))
    _interleave(staged(inv_stages, members[half:]),
                [fin_mean(g) for g in range(N_HG)] + [fin_out(g) for g in range(N_HG)])


def _stack_io(bufs, n_in, first_out):
    return ([pl.BlockSpec(memory_space=pl.ANY)] * len(bufs), list(bufs),
            {n_in + i: first_out + i for i in range(len(bufs))})


def _wkv_prompt(z3, vecs, w2a2, tri, pmat, gmask, stacked, layer):
    L = WKV_L
    n_chunks = SEQ // L
    cur = lambda s: jnp.minimum(s, n_chunks - 1)
    prv = lambda s: jnp.maximum(s - 1, 0)
    zspec = lambda cb: pl.BlockSpec((BATCH, L, D_R), lambda s: (0, cur(s), cb))
    full = lambda a: pl.BlockSpec(a.shape, lambda s: (0,) * a.ndim)
    row_scr = lambda w: pltpu.VMEM((BATCH, 1, w), F32)
    extra_specs, extra_ops, aliases = _stack_io(stacked, 10, 1)
    return pl.pallas_call(
        _wkv_prompt_kernel,
        grid=(n_chunks + 1,),
        in_specs=[
            zspec(CB_R), zspec(CB_K), zspec(CB_V),
            pl.BlockSpec((BATCH, L, 2 * R_W), lambda s: (0, cur(s), CB_WA)),
            pl.BlockSpec((BATCH, L, D_R), lambda s: (0, prv(s), CB_GR)),
            pl.BlockSpec((None, N_VEC, D_R), lambda s: (layer, 0, 0)),
            pl.BlockSpec((None, 2 * R_W, 2 * D_R), lambda s: (layer, 0, 0)),
            full(tri), full(pmat), full(gmask),
        ] + extra_specs,
        out_specs=[
            pl.BlockSpec((BATCH, L, D_R), lambda s: (0, prv(s), 0)),
            pl.BlockSpec((None, BATCH, H_R, HEAD, HEAD), lambda s: (layer, 0, 0, 0, 0)),
            pl.BlockSpec((None, BATCH, 1, D_SHIFT), lambda s: (layer, 0, 0, 0)),
        ],
        out_shape=[
            jax.ShapeDtypeStruct((BATCH, SEQ, D_R), BF16),
            jax.ShapeDtypeStruct((DEPTH, BATCH, H_R, HEAD, HEAD), F32),
            jax.ShapeDtypeStruct((DEPTH, BATCH, 1, D_SHIFT), F32),
        ],
        scratch_shapes=[
            pltpu.VMEM((BATCH, N_HG, GL, GL), F32),
            row_scr(D_R), row_scr(D_R), row_scr(D_R), row_scr(2 * R_W),
            pltpu.VMEM((BATCH * L, D_R), F32),
        ] + _chunk_buffers() + _chunk_buffers(),
        input_output_aliases=aliases,
        compiler_params=_params("arbitrary"),
        name="wkv_prompt",
    )(z3, z3, z3, z3, z3, vecs, w2a2, tri, pmat, gmask, *extra_ops)


def _wkv_sample_kernel(r_ref, k_ref, v_ref, gr_ref, wa_ref, sh_in_ref, s_in_ref,
                       vecs_ref, w2a2_ref, tri_ref, pmat_ref, gmask_ref,
                       y_ref, s_out_ref, sh_out_ref, seg_s, u_s, zd_s, ga_s, gr_s):
    nb, T = SAMPLE_NB, DEC_SEQ
    R = nb * T
    vecs = vecs_ref[...]
    pmat = pmat_ref[...]
    gmask = gmask_ref[...]
    bc = lambda x, j: _bcast_row(x, nb, j)

    r, k, v, wa = [ref[...].astype(F32) for ref in (r_ref, k_ref, v_ref, wa_ref)]
    cols = ((0, D_R), (D_R, 2 * D_R), (2 * D_R, 3 * D_R), (3 * D_R, D_SHIFT))
    prev = [_prev_rows(x, sh_in_ref[:, :, lo:hi], nb) for x, (lo, hi) in zip((r, k, v, wa), cols)]
    for x, (lo, hi) in zip((r, k, v, wa), cols):
        sh_out_ref[:, :, lo:hi] = _grouped(x, nb)[:, T - 1:T, :]
    ops = _wkv_prep(r, k, v, wa, *prev, vecs, w2a2_ref[...], tri_ref[...], pmat, lambda x: bc(x, T - 1))
    rt, kt, at, bt, vv = ops["rt"], ops["kt"], ops["at"], ops["bt"], ops["v"]
    zd_s[...] = jnp.concatenate([ops["bd"], ops["kd"]], axis=0)

    sl = lambda g: slice(g * GL, (g + 1) * GL)
    groups = [(b, g) for b in range(nb) for g in range(N_HG)]

    t_idx = _iota((R, D_R), 0) & (T - 1)
    acc = {"u": jnp.zeros((R, D_R), F32), "y": jnp.zeros((R, D_R), F32)}

    def score_piece(j):
        def run():
            btj, ktj, vj = bc(bt, j), bc(kt, j), bc(vv, j)
            seg = _segsum(jnp.concatenate([at * btj, at * ktj, rt * btj, rt * ktj], axis=0), pmat)
            seg_s[j, 0] = seg[0:R]
            seg_s[j, 1] = seg[2 * R:3 * R]
            acc["u"] = acc["u"] + jnp.where(t_idx > j, seg[R:2 * R] * vj, 0.0)
            acc["y"] = acc["y"] + jnp.where(t_idx >= j, seg[3 * R:4 * R] * vj, 0.0)
        return run

    def state_term_piece(b, g):
        def run():
            s4 = s_in_ref[b, g * HG:(g + 1) * HG].reshape(GL, HEAD)
            s_bd = jnp.concatenate([s4.astype(BF16)] * HG, axis=1) * gmask
            lhs = jnp.concatenate([at[b * T:(b + 1) * T, sl(g)], rt[b * T:(b + 1) * T, sl(g)]], axis=0)
            gs = _mm_nt(lhs, s_bd)
            ga_s[b * T:(b + 1) * T, sl(g)] = gs[:T]
            gr_s[b * T:(b + 1) * T, sl(g)] = gs[T:]
        return run

    _interleave([state_term_piece(b, g) for b, g in groups], [score_piece(j) for j in range(T)])

    u = ga_s[...] + acc["u"]
    for j in range(T - 1):
        u = u + jnp.where(t_idx > j, seg_s[j, 0] * bc(u, j), 0.0)
    u_s[...] = jnp.concatenate([u, vv], axis=0)
    wl = ops["wl"]
    acc["y"] = acc["y"] + gr_s[...]

    def y_piece(j):
        def run():
            acc["y"] = acc["y"] + jnp.where(t_idx >= j, seg_s[j, 1] * bc(u, j), 0.0)
        return run

    def finish_piece():
        y_ref[...] = _wkv_finish(acc["y"], ops["bonus"], gr_ref[...].astype(F32), vecs, pmat)

    row2 = _iota((2 * R, GL), 0) & (R - 1)
    uv_t = {}

    def state_piece(g, b):
        def run():
            if b == 0:
                uv_t[g] = u_s[:, sl(g)].T.astype(BF16)
            mine = (row2 >= b * T) & (row2 < (b + 1) * T)
            upd = _dot(uv_t[g], jnp.where(mine, zd_s[:, sl(g)], 0.0).astype(BF16))
            for j in range(HG):
                h = g * HG + j
                s_out_ref[b, h] = (s_in_ref[b, h] * wl[b * T:b * T + 1, h * HEAD:(h + 1) * HEAD]
                                   + upd[j * HEAD:(j + 1) * HEAD, j * HEAD:(j + 1) * HEAD])
        return run

    _interleave([state_piece(g, b) for g in range(N_HG) for b in range(nb)],
                [y_piece(j) for j in range(T)] + [finish_piece])


def _wkv_sample(z, shift_buf, state_buf, vecs, w2a2, tri, pmat, gmask, layer):
    nb, T = SAMPLE_NB, DEC_SEQ
    R = nb * T
    zspec = lambda cb: pl.BlockSpec((R, D_R), lambda i: (i, cb))
    full = lambda a: pl.BlockSpec(a.shape, lambda i: (0,) * a.ndim)
    sspec = pl.BlockSpec((None, nb, H_R, HEAD, HEAD), lambda i: (layer, i, 0, 0, 0))
    shspec = pl.BlockSpec((None, nb, 1, D_SHIFT), lambda i: (layer, i, 0, 0))
    return pl.pallas_call(
        _wkv_sample_kernel,
        grid=(DEC_BATCH // nb,),
        in_specs=[
            zspec(CB_R), zspec(CB_K), zspec(CB_V), zspec(CB_GR),
            pl.BlockSpec((R, 2 * R_W), lambda i: (i, CB_WA)),
            shspec, sspec,
            pl.BlockSpec((None, N_VEC, D_R), lambda i: (layer, 0, 0)),
            pl.BlockSpec((None, 2 * R_W, 2 * D_R), lambda i: (layer, 0, 0)),
            full(tri), full(pmat), full(gmask),
        ],
        out_specs=[pl.BlockSpec((R, D_R), lambda i: (i, 0)), sspec, shspec],
        out_shape=[
            jax.ShapeDtypeStruct((N_SAMPLE, D_R), BF16),
            jax.ShapeDtypeStruct((DEPTH, DEC_BATCH, H_R, HEAD, HEAD), F32),
            jax.ShapeDtypeStruct((DEPTH, DEC_BATCH, 1, D_SHIFT), F32),
        ],
        scratch_shapes=[
            pltpu.VMEM((T, 2, R, D_R), F32),
            pltpu.VMEM((2 * R, D_R), F32),
            pltpu.VMEM((2 * R, D_R), F32),
            pltpu.VMEM((R, D_R), F32),
            pltpu.VMEM((R, D_R), F32),
        ],
        input_output_aliases={5: 2, 6: 1},
        compiler_params=_params("arbitrary"),
        name="wkv_sample",
    )(z, z, z, z, z, shift_buf, state_buf, vecs, w2a2, tri, pmat, gmask)


def _layer_norm(x, g, b):
    mu = jnp.mean(x, axis=-1, keepdims=True)
    d = x - mu
    var = jnp.mean(d * d, axis=-1, keepdims=True)
    return d * lax.rsqrt(var + LN_EPS) * g + b


def _sgu_sample_kernel(u_ref, vg_ref, gg_ref, lng_ref, lnb_ref, wc_ref, bias_ref, *rest):
    y_ref, vn_ref = rest[-2:]
    T = DEC_SEQ
    R = u_ref.shape[0]
    nb = R // T
    vn = _layer_norm(vg_ref[...].astype(F32), lng_ref[...], lnb_ref[...])
    vn_ref[...] = vn
    vn3 = vn.reshape(nb, T, D_G)
    s3 = jnp.broadcast_to(bias_ref[...][None], (nb, T, D_G))
    for j in range(T):
        s3 = s3 + jnp.broadcast_to(vn3[:, j:j + 1, :], vn3.shape) * wc_ref[j][None]
    gg = gg_ref[...].astype(F32)
    y_ref[...] = (u_ref[...].astype(F32) * s3.reshape(R, D_G) * (gg * jax.nn.sigmoid(gg))).astype(BF16)


def _sgu_sample(z, ln_g, ln_b, wc, bias_c, vn_buf, layer):
    R = 256
    zspec = lambda cb: pl.BlockSpec((R, D_G), lambda i: (i, cb))
    vspec = pl.BlockSpec((None, 1, D_G), lambda i: (layer, 0, 0))
    extra_specs, extra_ops, aliases = _stack_io((vn_buf,), 7, 1)
    return pl.pallas_call(
        _sgu_sample_kernel,
        grid=(N_SAMPLE // R,),
        in_specs=[
            zspec(CB_U), zspec(CB_VG), zspec(CB_GG), vspec, vspec,
            pl.BlockSpec((None, DEC_SEQ, DEC_SEQ, D_G), lambda i: (layer, 0, 0, 0)),
            pl.BlockSpec((None, DEC_SEQ, D_G), lambda i: (layer, 0, 0)),
        ] + extra_specs,
        out_specs=[pl.BlockSpec((R, D_G), lambda i: (i, 0)),
                   pl.BlockSpec((None, R, D_G), lambda i: (layer, i, 0))],
        out_shape=[jax.ShapeDtypeStruct((N_SAMPLE, D_G), BF16),
                   jax.ShapeDtypeStruct((DEPTH, N_SAMPLE, D_G), F32)],
        input_output_aliases=aliases,
        compiler_params=_params("arbitrary"),
        name="sgu_sample",
    )(z, z, z, ln_g, ln_b, wc, bias_c, *extra_ops)


def _outproj_tail(ya, yb_in, ga_ref, gb_ref, h_ref, wb_ref, wo_ref, gn_ref, outs):
    yb = _dot(yb_in, wb_ref[...])
    m = jax.nn.sigmoid(ga_ref[...].astype(F32)) * ya + jax.nn.sigmoid(gb_ref[...].astype(F32)) * yb
    h_new = h_ref[...] + _dot(m.astype(BF16), wo_ref[...])
    xn = _rms_scale(h_new, gn_ref[...])
    if len(outs) == 1:
        outs[0][...] = xn
    else:
        outs[0][...] = h_new
        outs[1][...] = xn.astype(BF16)


def _outproj_kernel(ya_ref, yb_ref, ga_ref, gb_ref, h_ref, wa_ref, wb_ref, wo_ref, gn_ref, *outs):
    _outproj_tail(_dot(ya_ref[...], wa_ref[...]), yb_ref[...], ga_ref, gb_ref, h_ref, wb_ref, wo_ref, gn_ref, outs)


def _outproj_sgu_kernel(ya_ref, u_ref, vg_ref, gg_ref, lng_ref, lnb_ref, sw_ref, sbias_ref,
                        ga_ref, gb_ref, h_ref, wa_ref, wb_ref, wo_ref, gn_ref, *outs):
    L = SGU_L
    n_col = 8
    cw = D_MODEL // n_col
    st = {}

    def ya_piece(c):
        def run():
            st["ya", c] = _dot(ya_ref[...], wa_ref[:, c * cw:(c + 1) * cw])
        return run

    causal = _iota((L, L), 0) >= _iota((L, L), 1)

    def sgu_norm(c):
        def run():
            st["vn", c] = _layer_norm(vg_ref[c * L:(c + 1) * L, :].astype(F32), lng_ref[...],
                                      lnb_ref[...]).astype(BF16)
        return run

    def sgu_mix(c):
        def run():
            parts = [_dot(jnp.where(causal, sw_ref[g], 0.0).astype(BF16), st["vn", c][:, g * GW:(g + 1) * GW])
                     for g in range(N_GROUPS)]
            gg = gg_ref[c * L:(c + 1) * L, :].astype(F32)
            st["yb", c] = (u_ref[c * L:(c + 1) * L, :].astype(F32) * (jnp.concatenate(parts, axis=1) + sbias_ref[...])
                           * (gg * jax.nn.sigmoid(gg))).astype(BF16)
        return run

    n_chunks = ya_ref.shape[0] // L
    _interleave([ya_piece(c) for c in range(n_col)],
                [f(c) for c in range(n_chunks) for f in (sgu_norm, sgu_mix)])
    ya = jnp.concatenate([st["ya", c] for c in range(n_col)], axis=1)
    yb_in = jnp.concatenate([st["yb", c] for c in range(n_chunks)], axis=0)
    _outproj_tail(ya, yb_in, ga_ref, gb_ref, h_ref, wb_ref, wo_ref, gn_ref, outs)


def _outproj(ya_in, yb_in, z, h, wpa, wpb, wout, gn, layer, last, sgu=None):
    n_tok = h.shape[0]
    tm = TM_OUT
    once = lambda shape, imap: pl.BlockSpec(shape, imap, pipeline_mode=pl.Buffered(1))
    wspec = lambda a: once(a.shape, lambda i: (0, 0))
    row = lambda w: pl.BlockSpec((tm, w), lambda i: (i, 0))
    zcol = lambda w, cb: pl.BlockSpec((tm, w), lambda i: (i, cb))
    tok = lambda dt: jax.ShapeDtypeStruct((n_tok, D_MODEL), dt)
    vspec = pl.BlockSpec((None, 1, D_G), lambda i: (layer, 0, 0))
    if sgu is None:
        kern, branch_specs, branch_ops = _outproj_kernel, [row(D_G)], (yb_in,)
    else:
        kern = _outproj_sgu_kernel
        branch_specs = [zcol(D_G, CB_U), zcol(D_G, CB_VG), zcol(D_G, CB_GG), vspec, vspec,
                        once((None, N_GROUPS, SGU_L, SGU_L), lambda i: (layer, 0, 0, 0)),
                        once((None, SGU_L, D_G), lambda i: (layer, 0, 0))]
        branch_ops = (z, z, z) + tuple(sgu)
    return pl.pallas_call(
        kern,
        grid=(n_tok // tm,),
        in_specs=[row(D_R)] + branch_specs + [
            zcol(D_MODEL, 0), zcol(D_MODEL, 1), row(D_MODEL),
            wspec(wpa), wspec(wpb), wspec(wout),
            pl.BlockSpec((None, 1, D_MODEL), lambda i: (gn[1], 0, 0)),
        ],
        out_specs=[row(D_MODEL)] if last else [row(D_MODEL), row(D_MODEL)],
        out_shape=[tok(F32)] if last else [tok(F32), tok(BF16)],
        compiler_params=pltpu.CompilerParams(dimension_semantics=("arbitrary",), vmem_limit_bytes=VMEM_LIMIT_PROJ),
        name="outproj",
    )(ya_in, *branch_ops, z, z, h, wpa, wpb, wout, gn[0])


def _block_tri(n, blk):
    i = jnp.arange(n)
    return ((i[:, None] >= i[None, :]) & (i[:, None] // blk == i[None, :] // blk)).astype(BF16)


def kernel(x_prompt, x_sample, state_wkv, state_shift, norm_g, w_in, shift_mu, w0, w2, a0, a2, k_k, k_a, r_k,
           lnx_g, lnx_b, sgu_ln_g, sgu_ln_b, sgu_w, sgu_b, w_proj_a, w_proj_b, w_out, final_norm_g):
    c_wd, c_gr = 3 * D_R, D_SHIFT
    proj_w = (w_proj_a, w_proj_b, w_out)

    zrow = jnp.zeros((DEPTH, D_R), F32)
    mu_wa = jnp.pad(shift_mu[:, c_wd:c_gr], ((0, 0), (0, D_R - 2 * R_W)))
    vec_rows = [shift_mu[:, 0:D_R], shift_mu[:, D_R:2 * D_R], shift_mu[:, 2 * D_R:3 * D_R], w0, a0, k_k, k_a,
                r_k.reshape(DEPTH, D_R), lnx_g, lnx_b, mu_wa] + [zrow] * (N_VEC - 11)
    vecs = jnp.stack(vec_rows, axis=1)
    zblk = jnp.zeros((DEPTH, R_W, D_R), F32)
    w2a2 = jnp.concatenate([jnp.concatenate([w2, zblk], axis=2),
                            jnp.concatenate([zblk, a2], axis=2)], axis=1).astype(BF16)
    norm_g3 = norm_g.reshape(DEPTH, 1, D_MODEL)
    final_g3 = final_norm_g.reshape(1, 1, D_MODEL)
    ln_g3, ln_b3 = sgu_ln_g.reshape(DEPTH, 1, D_G), sgu_ln_b.reshape(DEPTH, 1, D_G)
    bias_exp = jnp.repeat(jnp.swapaxes(sgu_b, 1, 2), GW, axis=2)
    tmask = jnp.tril(jnp.ones((DEC_SEQ, DEC_SEQ), F32))
    wc = jnp.repeat(jnp.transpose(sgu_w[:, :, :DEC_SEQ, :DEC_SEQ] * tmask, (0, 3, 2, 1)), GW, axis=3)
    bias_c = bias_exp[:, :DEC_SEQ, :]

    pmat = ((jnp.arange(GL)[:, None] // HEAD) == (jnp.arange(GL)[None, :] // HEAD)).astype(BF16)
    tri_p = _block_tri(WKV_L, WKV_L)
    tri_s = _block_tri(SAMPLE_NB * DEC_SEQ, DEC_SEQ)
    src_cols = jnp.asarray(Z_SRC_UNITS, jnp.int32)

    h_p = x_prompt.reshape(N_PROMPT, D_MODEL)
    h_s = x_sample.reshape(N_SAMPLE, D_MODEL)
    xn_p = xn_s = None
    stk_p = [jnp.zeros((DEPTH, BATCH, H_R, HEAD, HEAD), F32), jnp.zeros((DEPTH, BATCH, 1, D_SHIFT), F32)]
    wkv_s, shift_s = state_wkv, state_shift
    vn_s = jnp.zeros((DEPTH, N_SAMPLE, D_G), F32)
    for l in range(DEPTH):
        last = l == DEPTH - 1
        if l == 0:
            z_p, wpa, wpb, wout = _inproj(h_p, w_in, src_cols, l, TM_IN_NORM, norm_g3, cast=proj_w)
            z_s, = _inproj(h_s, w_in, src_cols, l, TM_IN_SAMPLE, norm_g3)
        else:
            z_p, wpa, wpb, wout = _inproj(xn_p, w_in, src_cols, l, TM_IN_PROMPT, cast=proj_w)
            z_s, = _inproj(xn_s, w_in, src_cols, l, TM_IN_SAMPLE)
        ya_p, *stk_p = _wkv_prompt(z_p.reshape(BATCH, SEQ, NZ), vecs, w2a2, tri_p, pmat, pmat, stk_p, l)
        ya_s, wkv_s, shift_s = _wkv_sample(z_s, shift_s, wkv_s, vecs, w2a2, tri_s, pmat, pmat, l)
        yb_s, vn_s = _sgu_sample(z_s, ln_g3, ln_b3, wc, bias_c, vn_s, l)
        gn = (final_g3, 0) if last else (norm_g3, l + 1)
        out_p = _outproj(ya_p.reshape(N_PROMPT, D_R), None, z_p, h_p, wpa, wpb, wout, gn, l, last,
                         sgu=(ln_g3, ln_b3, sgu_w, bias_exp))
        out_s = _outproj(ya_s, yb_s, z_s, h_s, wpa, wpb, wout, gn, l, last)
        if last:
            y_p, y_s = out_p[0], out_s[0]
        else:
            (h_p, xn_p), (h_s, xn_s) = out_p, out_s
    return (y_p.reshape(BATCH, SEQ, D_MODEL), y_s.reshape(DEC_BATCH, DEC_SEQ, D_MODEL),
            stk_p[0], stk_p[1], wkv_s, shift_s, vn_s.reshape(DEPTH, DEC_BATCH, DEC_SEQ, D_G))
```

```python
import math

import jax
import jax.numpy as jnp
from jax import lax
from jax.experimental import pallas as pl
from jax.experimental.pallas import tpu as pltpu

F32 = jnp.float32
BF16 = jnp.bfloat16

D_MODEL = 2048
BATCH = 4
SEQ = 2048
DEPTH = 4
DEC_BATCH = 128
DEC_SEQ = 8
HEAD = 64
D_R = 1024
H_R = 16
R_W = 64
R_A = 64
D_G = 1024
N_GROUPS = 8
GW = 128
D_SHIFT = 3 * D_R + R_W + R_A
RMS_EPS = 1e-6
LN_EPS = 1e-5
GN_EPS = 64e-5
EXP_M05 = math.exp(-0.5)

N_PROMPT = BATCH * SEQ
N_SAMPLE = DEC_BATCH * DEC_SEQ

LANE = 128
CB_U, CB_VG, CB_GG, CB_R, CB_K, CB_V, CB_GR = 4, 5, 6, 7, 8, 9, 10
CB_WA = 88
TN_IN = 512
NZ = 23 * TN_IN
Z_SRC_UNITS = tuple([57 + 4 * j for j in range(8)] + [33 + 4 * j for j in range(6)] + [4 * j for j in range(6)]
                    + [25, 29] + [24])

HG = 4
GL = HG * HEAD
N_HG = H_R // HG

WKV_L = 64
N_CHAIN = BATCH * N_HG
SAMPLE_NB = 8

TM_IN_PROMPT, TM_IN_SAMPLE = 4096, 1024
TM_IN_NORM = 2048
TM_OUT = 512
SGU_L = 128
VMEM_LIMIT = 56 * 1024 * 1024
VMEM_LIMIT_PROJ = 62 * 1024 * 1024

V_MU_R, V_MU_K, V_MU_V, V_W0, V_A0, V_KK, V_KA, V_RK, V_LNG, V_LNB, V_MU_WA = range(11)
N_VEC = 16


def _dot(a, b):
    return jnp.dot(a, b, preferred_element_type=F32)


def _mm_nt(a, b):
    return lax.dot_general(a.astype(BF16), b.astype(BF16), (((1,), (1,)), ((), ())),
                           preferred_element_type=F32)


def _mm_tn(a, b):
    return lax.dot_general(a.astype(BF16), b.astype(BF16), (((0,), (0,)), ((), ())),
                           preferred_element_type=F32)


def _iota(shape, dim):
    return lax.broadcasted_iota(jnp.int32, shape, dim)


def _block_diag(x, gmask_bf):
    reps = GL // x.shape[0]
    return jnp.concatenate([x.astype(BF16)] * reps, axis=0) * gmask_bf


def _segsum(x, pmat):
    rows, n = x.shape[0], x.shape[1] // GL
    xb = x.astype(BF16)
    stacked = jnp.concatenate([xb[:, g * GL:(g + 1) * GL] for g in range(n)], axis=0)
    s = _dot(stacked, pmat)
    return jnp.concatenate([s[g * rows:(g + 1) * rows] for g in range(n)], axis=1)


def _cumsum_rows(tri, x):
    hi = x.astype(BF16)
    lo = (x - hi.astype(F32)).astype(BF16)
    return _dot(tri, hi) + _dot(tri, lo)


def _rms_scale(x, g):
    ms = jnp.mean(x * x, axis=-1, keepdims=True)
    return x * lax.rsqrt(ms + RMS_EPS) * g


def _params(*sem):
    return pltpu.CompilerParams(dimension_semantics=sem, vmem_limit_bytes=VMEM_LIMIT)


CAST_ROWS = 64
N_CAST_A = D_R // CAST_ROWS
N_CAST_O = D_MODEL // CAST_ROWS


def _inproj_body(x_bf, w_ref, z_ref, casts):
    z_ref[...] = _dot(x_bf, w_ref[...].astype(BF16)).astype(z_ref.dtype)
    if casts:
        wa_ref, wb_ref, wo_ref, wa_out, wb_out, wo_out = casts
        step = pl.program_id(0) * pl.num_programs(1) + pl.program_id(1)

        @pl.when(step < N_CAST_A)
        def _():
            wa_out[...] = wa_ref[...].astype(BF16)
            wb_out[...] = wb_ref[...].astype(BF16)

        @pl.when(step < N_CAST_O)
        def _():
            wo_out[...] = wo_ref[...].astype(BF16)


def _inproj_kernel(cols_ref, x_ref, w_ref, *rest):
    del cols_ref
    _inproj_body(x_ref[...], w_ref, rest[-4] if len(rest) > 1 else rest[0],
                 rest[:3] + rest[-3:] if len(rest) > 1 else None)


def _inproj_norm_kernel(cols_ref, x_ref, g_ref, w_ref, *rest):
    del cols_ref
    xn_ref = rest[-1]

    @pl.when(pl.program_id(1) == 0)
    def _():
        xn_ref[...] = _rms_scale(x_ref[...], g_ref[...]).astype(BF16)

    rest = rest[:-1]
    _inproj_body(xn_ref[...], w_ref, rest[-4] if len(rest) > 1 else rest[0],
                 rest[:3] + rest[-3:] if len(rest) > 1 else None)


def _inproj(x, w_in, src_cols, layer, tm, norm_g3=None, cast=None):
    n_tok = x.shape[0]
    n_j = NZ // TN_IN
    wspec = pl.BlockSpec((None, pl.Element(D_MODEL), pl.Element(TN_IN)),
                         lambda i, j, units: (layer, 0, units[j] * LANE))
    if norm_g3 is None:
        kern, scratch, operands = _inproj_kernel, [], [x, w_in]
        in_specs = [pl.BlockSpec((tm, D_MODEL), lambda i, j, units: (i, 0)), wspec]
    else:
        kern, scratch, operands = _inproj_norm_kernel, [pltpu.VMEM((tm, D_MODEL), BF16)], [x, norm_g3, w_in]
        in_specs = [pl.BlockSpec((tm, D_MODEL), lambda i, j, units: (i, 0)),
                    pl.BlockSpec((None, 1, D_MODEL), lambda i, j, units: (layer, 0, 0)), wspec]
    out_specs = [pl.BlockSpec((tm, TN_IN), lambda i, j, units: (i, j))]
    out_shape = [jax.ShapeDtypeStruct((n_tok, NZ), BF16)]
    if cast is not None:
        assert (n_tok // tm) * n_j >= N_CAST_O
        for w, n_blk in zip(cast, (N_CAST_A, N_CAST_A, N_CAST_O)):
            blk = lambda i, j, units, n_blk=n_blk: jnp.minimum(i * n_j + j, n_blk - 1)
            in_specs.append(pl.BlockSpec((None, CAST_ROWS, D_MODEL),
                                         lambda i, j, units, blk=blk: (layer, blk(i, j, units), 0)))
            out_specs.append(pl.BlockSpec((CAST_ROWS, D_MODEL), lambda i, j, units, blk=blk: (blk(i, j, units), 0)))
            out_shape.append(jax.ShapeDtypeStruct(w.shape[1:], BF16))
            operands.append(w)
    grid_spec = pltpu.PrefetchScalarGridSpec(
        num_scalar_prefetch=1,
        grid=(n_tok // tm, n_j),
        in_specs=in_specs,
        out_specs=out_specs,
        scratch_shapes=scratch,
    )
    return pl.pallas_call(
        kern,
        grid_spec=grid_spec,
        out_shape=out_shape,
        compiler_params=pltpu.CompilerParams(dimension_semantics=("arbitrary", "arbitrary"),
                                             vmem_limit_bytes=VMEM_LIMIT_PROJ),
        name="inproj",
    )(src_cols, *operands)


def _shift_mix(x, x_prev, mu):
    return x + (x_prev - x) * mu


def _lowrank_in(xwa):
    return jnp.where(_iota(xwa.shape, 1) < R_W, jnp.tanh(xwa), xwa).astype(BF16)


def _prep_gates(xr, xk, lr_w, lr_a, vecs):
    vec = lambda i: vecs[i:i + 1, :]
    lw = -EXP_M05 * jax.nn.sigmoid(vec(V_W0) + lr_w)
    ag = jax.nn.sigmoid(vec(V_A0) + lr_a)
    kk = xk * vec(V_KK)
    k2 = xk * (1.0 + (ag - 1.0) * vec(V_KA))
    return dict(lw=lw, ag=ag, kk=kk, k2=k2, seg_in=jnp.concatenate([kk * kk, xr * k2 * vec(V_RK)], axis=0))


def _prep_scale(xr, xv, gt, cl, seg, last_row):
    rows = cl.shape[0]
    kkn = gt["kk"] / jnp.maximum(jnp.sqrt(seg[:rows]), 1e-12)
    cl_last = last_row(cl)
    e_neg = jnp.exp(-cl)
    e_rem = jnp.exp(cl_last - cl)
    bb = kkn * gt["ag"]
    return dict(
        rt=xr * jnp.exp(cl),
        kt=gt["k2"] * e_neg,
        at=-kkn * jnp.exp(cl - gt["lw"]),
        bt=bb * e_neg,
        kd=gt["k2"] * e_rem,
        bd=bb * e_rem,
        v=xv,
        wl=jnp.exp(cl_last),
        bonus=seg[rows:] * xv,
    )


def _wkv_prep(r, k, v, wa, r_prev, k_prev, v_prev, wa_prev, vecs, w2a2, tri, pmat, last_row):
    vec = lambda i: vecs[i:i + 1, :]
    xr = _shift_mix(r, r_prev, vec(V_MU_R))
    xk = _shift_mix(k, k_prev, vec(V_MU_K))
    xv = _shift_mix(v, v_prev, vec(V_MU_V))
    lowrank = _dot(_lowrank_in(_shift_mix(wa, wa_prev, vecs[V_MU_WA:V_MU_WA + 1, :2 * R_W])), w2a2)
    gt = _prep_gates(xr, xk, lowrank[:, :D_R], lowrank[:, D_R:], vecs)
    return _prep_scale(xr, xv, gt, _cumsum_rows(tri, gt["lw"]), _segsum(gt["seg_in"], pmat), last_row)


def _wkv_finish(y, bonus, g_r, vecs, pmat):
    mu = _segsum(y, pmat) * (1.0 / HEAD)
    d = y - mu
    var = _segsum(d * d, pmat) * (1.0 / HEAD)
    yn = d * lax.rsqrt(var + GN_EPS) * vecs[V_LNG:V_LNG + 1, :] + vecs[V_LNB:V_LNB + 1, :]
    return ((yn + bonus) * (g_r * jax.nn.sigmoid(g_r))).astype(BF16)


def _grouped(x, nb):
    return x.reshape(nb, x.shape[0] // nb, x.shape[-1])


def _bcast_row(x, nb, j):
    x3 = _grouped(x, nb)
    return jnp.broadcast_to(x3[:, j:j + 1, :], x3.shape).reshape(x.shape)


def _prev_rows(x, prev, nb):
    t = x.shape[0] // nb
    first = (_iota(x.shape, 0) & (t - 1)) == 0
    return jnp.where(first, jnp.broadcast_to(prev, (nb, t, x.shape[-1])).reshape(x.shape), pltpu.roll(x, 1, 0))


def _interleave(major, minor):
    done = 0
    for i, piece in enumerate(major):
        piece()
        upto = (i + 1) * len(minor) // len(major)
        for other in minor[done:upto]:
            other()
        done = upto
    for other in minor[done:]:
        other()


def _chunk_buffers():
    L = WKV_L
    return [
        pltpu.VMEM((N_CHAIN, 2 * L, GL), BF16),
        pltpu.VMEM((N_CHAIN, L, GL), BF16),
        pltpu.VMEM((N_CHAIN, 2 * L, GL), F32),
        pltpu.VMEM((N_CHAIN, L, GL), BF16),
        pltpu.VMEM((N_CHAIN, 2 * L, GL), BF16),
        pltpu.VMEM((BATCH * L, D_R), F32),
        pltpu.VMEM((BATCH * L, D_R), F32),
        pltpu.VMEM((BATCH, 8, D_R), F32),
    ]


N_CHUNK_BUF = 8


def _wkv_prompt_kernel(r_ref, k_ref, v_ref, wa_ref, gr_ref, vecs_ref, w2a2_ref, tri_ref, pmat_ref, gmask_ref,
                       *rest):
    n_scr = 6 + 2 * N_CHUNK_BUF
    y_ref, s_out_ref, sh_out_ref = rest[-n_scr - 3:-n_scr]
    s_ref, pr_ref, pk_ref, pv_ref, pwa_ref, y_s = rest[-n_scr:-n_scr + 6]
    set_a = rest[-2 * N_CHUNK_BUF:-N_CHUNK_BUF]
    set_b = rest[-N_CHUNK_BUF:]
    prevs = (pr_ref, pk_ref, pv_ref, pwa_ref)
    s = pl.program_id(0)

    @pl.when(s == 0)
    def _():
        s_ref[...] = jnp.zeros_like(s_ref)
        for ref in prevs + tuple(set_b):
            ref[...] = jnp.zeros_like(ref)

    args = (r_ref, k_ref, v_ref, wa_ref, gr_ref, vecs_ref, w2a2_ref, tri_ref, pmat_ref, gmask_ref,
            y_ref, s_ref, prevs, y_s)

    @pl.when((s & 1) == 0)
    def _():
        _wkv_prompt_step(*args, set_a, set_b)

    @pl.when((s & 1) == 1)
    def _():
        _wkv_prompt_step(*args, set_b, set_a)

    @pl.when(s == pl.num_programs(0) - 1)
    def _():
        for b in range(BATCH):
            for g in range(N_HG):
                for j in range(HG):
                    s_out_ref[b, g * HG + j] = s_ref[b, g, j * HEAD:(j + 1) * HEAD, :][:, j * HEAD:(j + 1) * HEAD]
        sh_out_ref[:, :, 0:D_R] = pr_ref[...]
        sh_out_ref[:, :, D_R:2 * D_R] = pk_ref[...]
        sh_out_ref[:, :, 2 * D_R:3 * D_R] = pv_ref[...]
        sh_out_ref[:, :, 3 * D_R:D_SHIFT] = pwa_ref[...]


def _wkv_prompt_step(r_ref, k_ref, v_ref, wa_ref, gr_ref, vecs_ref, w2a2_ref, tri_ref, pmat_ref, gmask_ref,
                     y_ref, s_ref, prevs, y_s, wset, rset):
    L, nb = WKV_L, BATCH
    vecs = vecs_ref[...]
    pmat = pmat_ref[...]
    gmask = gmask_ref[...]
    gmask_f = gmask.astype(F32)
    tri = tri_ref[...]
    rs = lambda b: slice(b * L, (b + 1) * L)
    sl = lambda g: slice(g * GL, (g + 1) * GL)
    chains = [(b, g) for b in range(nb) for g in range(N_HG)]
    lhs_r, p_r, av_r, arb_r, zd_r, v_r, bonus_r, wl_r = rset
    lhs_w, p_w, av_w, arb_w, zd_w, v_w, bonus_w, wl_w = wset
    pr_ref, pk_ref, pv_ref, pwa_ref = prevs

    st = {}

    def b_state_terms(n, b, g):
        st["gs", n] = _mm_nt(lhs_r[n], s_ref[b, g].astype(BF16) * gmask)

    def b_u(n, b, g):
        st["u", n] = _dot(p_r[n], _block_diag(st["gs", n][:L] + av_r[n, 0:L, :], gmask))

    def b_y(n, b, g):
        y_s[rs(b), sl(g)] = st["gs", n][L:] + av_r[n, L:2 * L, :] + _dot(arb_r[n], _block_diag(st["u", n], gmask))

    def b_state(n, b, g):
        uv = jnp.concatenate([st["u", n], v_r[rs(b), sl(g)]], axis=0)
        s_ref[b, g] = s_ref[b, g] * wl_r[b, 0:1, sl(g)] + _mm_tn(uv, zd_r[n])

    def staged(fns, members):
        return [(lambda f=f, n=n, b=b, g=g: f(n, b, g)) for f in fns for n, (b, g) in members]

    state_part = staged([b_state_terms, b_u, b_y, b_state], list(enumerate(chains)))

    row = _iota((L, GL), 0)
    col = _iota((L, GL), 1) & (HEAD - 1)
    strict = row > col
    incl = row >= col
    row2 = _iota((2 * L, GL), 0)
    col2 = _iota((2 * L, GL), 1) & (HEAD - 1)
    both = (row2 - jnp.where(row2 < L, 0, L - 1)) > col2
    eye = jnp.where(row == col, 1.0, 0.0)
    pp = {}

    def prep_pieces(b):
        def mix_r():
            x = r_ref[b].astype(F32)
            pp["xr", b] = _shift_mix(x, _prev_rows(x, pr_ref[b:b + 1], 1), vecs[V_MU_R:V_MU_R + 1, :])
            pr_ref[b] = x[L - 1:L, :]

        def mix_k():
            x = k_ref[b].astype(F32)
            pp["xk", b] = _shift_mix(x, _prev_rows(x, pk_ref[b:b + 1], 1), vecs[V_MU_K:V_MU_K + 1, :])
            pk_ref[b] = x[L - 1:L, :]

        def mix_v():
            x = v_ref[b].astype(F32)
            xv = _shift_mix(x, _prev_rows(x, pv_ref[b:b + 1], 1), vecs[V_MU_V:V_MU_V + 1, :])
            pp["xv", b] = xv
            v_w[rs(b), :] = xv
            pv_ref[b] = x[L - 1:L, :]

        def lowrank():
            x = wa_ref[b].astype(F32)
            xwa = _shift_mix(x, _prev_rows(x, pwa_ref[b:b + 1], 1), vecs[V_MU_WA:V_MU_WA + 1, :2 * R_W])
            pwa_ref[b] = x[L - 1:L, :]
            pp["lr", b] = _dot(_lowrank_in(xwa), w2a2_ref[...])

        def gates(g):
            def run():
                xr, xk = pp["xr", b][:, sl(g)], pp["xk", b][:, sl(g)]
                lr = pp["lr", b]
                gt = _prep_gates(xr, xk, lr[:, sl(g)], lr[:, D_R + g * GL:D_R + (g + 1) * GL], vecs[:, sl(g)])
                pp["gt", b, g] = gt
                pp["cl", b, g] = _cumsum_rows(tri, gt["lw"])
                pp["seg", b, g] = _segsum(gt["seg_in"], pmat)
            return run

        def scale(g):
            def run():
                n = b * N_HG + g
                ops = _prep_scale(pp["xr", b][:, sl(g)], pp["xv", b][:, sl(g)], pp["gt", b, g], pp["cl", b, g],
                                  pp["seg", b, g], lambda x: jnp.broadcast_to(x[L - 1:L, :], x.shape))
                bonus_w[rs(b), sl(g)] = ops["bonus"]
                wl_w[b, :, sl(g)] = ops["wl"][0:8, :]
                lhs = jnp.concatenate([ops["at"], ops["rt"]], axis=0).astype(BF16)
                lhs_w[n] = lhs
                zd_w[n] = jnp.concatenate([ops["bd"], ops["kd"]], axis=0).astype(BF16)
                pp["lhs", n] = lhs
                pp["bk", n] = jnp.concatenate([_block_diag(ops["bt"], gmask), _block_diag(ops["kt"], gmask)],
                                              axis=0)
            return run

        return ([mix_r, mix_k, mix_v, lowrank] + [gates(g) for g in range(N_HG)]
                + [scale(g) for g in range(N_HG)])

    def a_scores(n, b, g):
        sc = _mm_nt(pp["lhs", n], pp["bk", n])
        a_ab = jnp.where(strict, sc[:L, :GL], 0.0)
        arb_w[n] = jnp.where(incl, sc[L:, :GL], 0.0).astype(BF16)
        pp["akv", n] = jnp.where(both, sc[:, GL:], 0.0).astype(BF16)
        pp["x", n] = a_ab.astype(BF16)
        pp["xbd", n] = _block_diag(a_ab, gmask)
        pp["p", n] = eye + a_ab

    def a_values(n, b, g):
        av_w[n] = _dot(pp["akv", n], _block_diag(pp["xv", b][:, sl(g)], gmask))

    def a_square(n, b, g):
        x = _dot(pp["x", n], pp["xbd", n]).astype(BF16)
        pp["x", n], pp["xbd", n] = x, _block_diag(x, gmask)

    def a_double(n, b, g):
        xp = _dot(jnp.concatenate([pp["x", n], pp["p", n].astype(BF16)], axis=0), pp["xbd", n])
        x = xp[:L].astype(BF16)
        pp["x", n], pp["xbd", n] = x, _block_diag(x, gmask)
        pp["p", n] = pp["p", n] + xp[L:]

    def a_inverse(n, b, g):
        p_w[n] = (pp["p", n] + _dot(pp["p", n].astype(BF16), pp["xbd", n])).astype(BF16)

    inv_stages = [a_scores, a_values, a_square] + [a_double] * (int(math.log2(L)) - 2) + [a_inverse]
    half = N_CHAIN // 2
    members = list(enumerate(chains))

    fin = {}

    def fin_mean(g):
        def run():
            y = y_s[:, sl(g)]
            fin[g] = y - _segsum(y, pmat) * (1.0 / HEAD)
        return run

    def fin_out(g):
        def run():
            d = fin[g]
            var = _segsum(d * d, pmat) * (1.0 / HEAD)
            yn = d * lax.rsqrt(var + GN_EPS) * vecs[V_LNG:V_LNG + 1, sl(g)] + vecs[V_LNB:V_LNB + 1, sl(g)]
            g_r = gr_ref[:, :, sl(g)].astype(F32).reshape(nb * L, GL)
            out = (yn + bonus_r[:, sl(g)]) * (g_r * jax.nn.sigmoid(g_r))
            y_ref[:, :, sl(g)] = out.astype(BF16).reshape(nb, L, GL)
        return run

    _interleave(state_part, prep_pieces(0) + prep_pieces(1))
    _interleave(staged(inv_stages, members[:half]), prep_pieces(2) + prep_pieces(3))
    _interleave(staged(inv_stages, members[half:]),
                [fin_mean(g) for g in range(N_HG)] + [fin_out(g) for g in range(N_HG)])


def _stack_io(bufs, n_in, first_out):
    return ([pl.BlockSpec(memory_space=pl.ANY)] * len(bufs), list(bufs),
            {n_in + i: first_out + i for i in range(len(bufs))})


def _wkv_prompt(z3, vecs, w2a2, tri, pmat, gmask, stacked, layer):
    L = WKV_L
    n_chunks = SEQ // L
    cur = lambda s: jnp.minimum(s, n_chunks - 1)
    prv = lambda s: jnp.maximum(s - 1, 0)
    zspec = lambda cb: pl.BlockSpec((BATCH, L, D_R), lambda s: (0, cur(s), cb))
    full = lambda a: pl.BlockSpec(a.shape, lambda s: (0,) * a.ndim)
    row_scr = lambda w: pltpu.VMEM((BATCH, 1, w), F32)
    extra_specs, extra_ops, aliases = _stack_io(stacked, 10, 1)
    return pl.pallas_call(
        _wkv_prompt_kernel,
        grid=(n_chunks + 1,),
        in_specs=[
            zspec(CB_R), zspec(CB_K), zspec(CB_V),
            pl.BlockSpec((BATCH, L, 2 * R_W), lambda s: (0, cur(s), CB_WA)),
            pl.BlockSpec((BATCH, L, D_R), lambda s: (0, prv(s), CB_GR)),
            pl.BlockSpec((None, N_VEC, D_R), lambda s: (layer, 0, 0)),
            pl.BlockSpec((None, 2 * R_W, 2 * D_R), lambda s: (layer, 0, 0)),
            full(tri), full(pmat), full(gmask),
        ] + extra_specs,
        out_specs=[
            pl.BlockSpec((BATCH, L, D_R), lambda s: (0, prv(s), 0)),
            pl.BlockSpec((None, BATCH, H_R, HEAD, HEAD), lambda s: (layer, 0, 0, 0, 0)),
            pl.BlockSpec((None, BATCH, 1, D_SHIFT), lambda s: (layer, 0, 0, 0)),
        ],
        out_shape=[
            jax.ShapeDtypeStruct((BATCH, SEQ, D_R), BF16),
            jax.ShapeDtypeStruct((DEPTH, BATCH, H_R, HEAD, HEAD), F32),
            jax.ShapeDtypeStruct((DEPTH, BATCH, 1, D_SHIFT), F32),
        ],
        scratch_shapes=[
            pltpu.VMEM((BATCH, N_HG, GL, GL), F32),
            row_scr(D_R), row_scr(D_R), row_scr(D_R), row_scr(2 * R_W),
            pltpu.VMEM((BATCH * L, D_R), F32),
        ] + _chunk_buffers() + _chunk_buffers(),
        input_output_aliases=aliases,
        compiler_params=_params("arbitrary"),
        name="wkv_prompt",
    )(z3, z3, z3, z3, z3, vecs, w2a2, tri, pmat, gmask, *extra_ops)


def _wkv_sample_kernel(r_ref, k_ref, v_ref, gr_ref, wa_ref, sh_in_ref, s_in_ref,
                       vecs_ref, w2a2_ref, tri_ref, pmat_ref, gmask_ref,
                       y_ref, s_out_ref, sh_out_ref, seg_s, u_s, zd_s, ga_s, gr_s):
    nb, T = SAMPLE_NB, DEC_SEQ
    R = nb * T
    vecs = vecs_ref[...]
    pmat = pmat_ref[...]
    gmask = gmask_ref[...]
    bc = lambda x, j: _bcast_row(x, nb, j)

    r, k, v, wa = [ref[...].astype(F32) for ref in (r_ref, k_ref, v_ref, wa_ref)]
    cols = ((0, D_R), (D_R, 2 * D_R), (2 * D_R, 3 * D_R), (3 * D_R, D_SHIFT))
    prev = [_prev_rows(x, sh_in_ref[:, :, lo:hi], nb) for x, (lo, hi) in zip((r, k, v, wa), cols)]
    for x, (lo, hi) in zip((r, k, v, wa), cols):
        sh_out_ref[:, :, lo:hi] = _grouped(x, nb)[:, T - 1:T, :]
    ops = _wkv_prep(r, k, v, wa, *prev, vecs, w2a2_ref[...], tri_ref[...], pmat, lambda x: bc(x, T - 1))
    rt, kt, at, bt, vv = ops["rt"], ops["kt"], ops["at"], ops["bt"], ops["v"]
    zd_s[...] = jnp.concatenate([ops["bd"], ops["kd"]], axis=0)

    sl = lambda g: slice(g * GL, (g + 1) * GL)
    groups = [(b, g) for b in range(nb) for g in range(N_HG)]

    t_idx = _iota((R, D_R), 0) & (T - 1)
    acc = {"u": jnp.zeros((R, D_R), F32), "y": jnp.zeros((R, D_R), F32)}

    def score_piece(j):
        def run():
            btj, ktj, vj = bc(bt, j), bc(kt, j), bc(vv, j)
            seg = _segsum(jnp.concatenate([at * btj, at * ktj, rt * btj, rt * ktj], axis=0), pmat)
            seg_s[j, 0] = seg[0:R]
            seg_s[j, 1] = seg[2 * R:3 * R]
            acc["u"] = acc["u"] + jnp.where(t_idx > j, seg[R:2 * R] * vj, 0.0)
            acc["y"] = acc["y"] + jnp.where(t_idx >= j, seg[3 * R:4 * R] * vj, 0.0)
        return run

    def state_term_piece(b, g):
        def run():
            s4 = s_in_ref[b, g * HG:(g + 1) * HG].reshape(GL, HEAD)
            s_bd = jnp.concatenate([s4.astype(BF16)] * HG, axis=1) * gmask
            lhs = jnp.concatenate([at[b * T:(b + 1) * T, sl(g)], rt[b * T:(b + 1) * T, sl(g)]], axis=0)
            gs = _mm_nt(lhs, s_bd)
            ga_s[b * T:(b + 1) * T, sl(g)] = gs[:T]
            gr_s[b * T:(b + 1) * T, sl(g)] = gs[T:]
        return run

    _interleave([state_term_piece(b, g) for b, g in groups], [score_piece(j) for j in range(T)])

    u = ga_s[...] + acc["u"]
    for j in range(T - 1):
        u = u + jnp.where(t_idx > j, seg_s[j, 0] * bc(u, j), 0.0)
    u_s[...] = jnp.concatenate([u, vv], axis=0)
    wl = ops["wl"]
    acc["y"] = acc["y"] + gr_s[...]

    def y_piece(j):
        def run():
            acc["y"] = acc["y"] + jnp.where(t_idx >= j, seg_s[j, 1] * bc(u, j), 0.0)
        return run

    def finish_piece():
        y_ref[...] = _wkv_finish(acc["y"], ops["bonus"], gr_ref[...].astype(F32), vecs, pmat)

    row2 = _iota((2 * R, GL), 0) & (R - 1)
    uv_t = {}

    def state_piece(g, b):
        def run():
            if b == 0:
                uv_t[g] = u_s[:, sl(g)].T.astype(BF16)
            mine = (row2 >= b * T) & (row2 < (b + 1) * T)
            upd = _dot(uv_t[g], jnp.where(mine, zd_s[:, sl(g)], 0.0).astype(BF16))
            for j in range(HG):
                h = g * HG + j
                s_out_ref[b, h] = (s_in_ref[b, h] * wl[b * T:b * T + 1, h * HEAD:(h + 1) * HEAD]
                                   + upd[j * HEAD:(j + 1) * HEAD, j * HEAD:(j + 1) * HEAD])
        return run

    _interleave([state_piece(g, b) for g in range(N_HG) for b in range(nb)],
                [y_piece(j) for j in range(T)] + [finish_piece])


def _wkv_sample(z, shift_buf, state_buf, vecs, w2a2, tri, pmat, gmask, layer):
    nb, T = SAMPLE_NB, DEC_SEQ
    R = nb * T
    zspec = lambda cb: pl.BlockSpec((R, D_R), lambda i: (i, cb))
    full = lambda a: pl.BlockSpec(a.shape, lambda i: (0,) * a.ndim)
    sspec = pl.BlockSpec((None, nb, H_R, HEAD, HEAD), lambda i: (layer, i, 0, 0, 0))
    shspec = pl.BlockSpec((None, nb, 1, D_SHIFT), lambda i: (layer, i, 0, 0))
    return pl.pallas_call(
        _wkv_sample_kernel,
        grid=(DEC_BATCH // nb,),
        in_specs=[
            zspec(CB_R), zspec(CB_K), zspec(CB_V), zspec(CB_GR),
            pl.BlockSpec((R, 2 * R_W), lambda i: (i, CB_WA)),
            shspec, sspec,
            pl.BlockSpec((None, N_VEC, D_R), lambda i: (layer, 0, 0)),
            pl.BlockSpec((None, 2 * R_W, 2 * D_R), lambda i: (layer, 0, 0)),
            full(tri), full(pmat), full(gmask),
        ],
        out_specs=[pl.BlockSpec((R, D_R), lambda i: (i, 0)), sspec, shspec],
        out_shape=[
            jax.ShapeDtypeStruct((N_SAMPLE, D_R), BF16),
            jax.ShapeDtypeStruct((DEPTH, DEC_BATCH, H_R, HEAD, HEAD), F32),
            jax.ShapeDtypeStruct((DEPTH, DEC_BATCH, 1, D_SHIFT), F32),
        ],
        scratch_shapes=[
            pltpu.VMEM((T, 2, R, D_R), F32),
            pltpu.VMEM((2 * R, D_R), F32),
            pltpu.VMEM((2 * R, D_R), F32),
            pltpu.VMEM((R, D_R), F32),
            pltpu.VMEM((R, D_R), F32),
        ],
        input_output_aliases={5: 2, 6: 1},
        compiler_params=_params("arbitrary"),
        name="wkv_sample",
    )(z, z, z, z, z, shift_buf, state_buf, vecs, w2a2, tri, pmat, gmask)


def _layer_norm(x, g, b):
    mu = jnp.mean(x, axis=-1, keepdims=True)
    d = x - mu
    var = jnp.mean(d * d, axis=-1, keepdims=True)
    return d * lax.rsqrt(var + LN_EPS) * g + b


def _sgu_sample_kernel(u_ref, vg_ref, gg_ref, lng_ref, lnb_ref, wc_ref, bias_ref, *rest):
    y_ref, vn_ref = rest[-2:]
    T = DEC_SEQ
    R = u_ref.shape[0]
    nb = R // T
    vn = _layer_norm(vg_ref[...].astype(F32), lng_ref[...], lnb_ref[...])
    vn_ref[...] = vn
    vn3 = vn.reshape(nb, T, D_G)
    s3 = jnp.broadcast_to(bias_ref[...][None], (nb, T, D_G))
    for j in range(T):
        s3 = s3 + jnp.broadcast_to(vn3[:, j:j + 1, :], vn3.shape) * wc_ref[j][None]
    gg = gg_ref[...].astype(F32)
    y_ref[...] = (u_ref[...].astype(F32) * s3.reshape(R, D_G) * (gg * jax.nn.sigmoid(gg))).astype(BF16)


def _sgu_sample(z, ln_g, ln_b, wc, bias_c, vn_buf, layer):
    R = 256
    zspec = lambda cb: pl.BlockSpec((R, D_G), lambda i: (i, cb))
    vspec = pl.BlockSpec((None, 1, D_G), lambda i: (layer, 0, 0))
    extra_specs, extra_ops, aliases = _stack_io((vn_buf,), 7, 1)
    return pl.pallas_call(
        _sgu_sample_kernel,
        grid=(N_SAMPLE // R,),
        in_specs=[
            zspec(CB_U), zspec(CB_VG), zspec(CB_GG), vspec, vspec,
            pl.BlockSpec((None, DEC_SEQ, DEC_SEQ, D_G), lambda i: (layer, 0, 0, 0)),
            pl.BlockSpec((None, DEC_SEQ, D_G), lambda i: (layer, 0, 0)),
        ] + extra_specs,
        out_specs=[pl.BlockSpec((R, D_G), lambda i: (i, 0)),
                   pl.BlockSpec((None, R, D_G), lambda i: (layer, i, 0))],
        out_shape=[jax.ShapeDtypeStruct((N_SAMPLE, D_G), BF16),
                   jax.ShapeDtypeStruct((DEPTH, N_SAMPLE, D_G), F32)],
        input_output_aliases=aliases,
        compiler_params=_params("arbitrary"),
        name="sgu_sample",
    )(z, z, z, ln_g, ln_b, wc, bias_c, *extra_ops)


def _outproj_tail(ya, yb_in, ga_ref, gb_ref, h_ref, wb_ref, wo_ref, gn_ref, outs):
    yb = _dot(yb_in, wb_ref[...])
    m = jax.nn.sigmoid(ga_ref[...].astype(F32)) * ya + jax.nn.sigmoid(gb_ref[...].astype(F32)) * yb
    h_new = h_ref[...] + _dot(m.astype(BF16), wo_ref[...])
    xn = _rms_scale(h_new, gn_ref[...])
    if len(outs) == 1:
        outs[0][...] = xn
    else:
        outs[0][...] = h_new
        outs[1][...] = xn.astype(BF16)


def _outproj_kernel(ya_ref, yb_ref, ga_ref, gb_ref, h_ref, wa_ref, wb_ref, wo_ref, gn_ref, *outs):
    _outproj_tail(_dot(ya_ref[...], wa_ref[...]), yb_ref[...], ga_ref, gb_ref, h_ref, wb_ref, wo_ref, gn_ref, outs)


def _outproj_sgu_kernel(ya_ref, u_ref, vg_ref, gg_ref, lng_ref, lnb_ref, sw_ref, sbias_ref,
                        ga_ref, gb_ref, h_ref, wa_ref, wb_ref, wo_ref, gn_ref, *outs):
    L = SGU_L
    n_col = 8
    cw = D_MODEL // n_col
    st = {}

    def ya_piece(c):
        def run():
            st["ya", c] = _dot(ya_ref[...], wa_ref[:, c * cw:(c + 1) * cw])
        return run

    causal = _iota((L, L), 0) >= _iota((L, L), 1)

    def sgu_norm(c):
        def run():
            st["vn", c] = _layer_norm(vg_ref[c * L:(c + 1) * L, :].astype(F32), lng_ref[...],
                                      lnb_ref[...]).astype(BF16)
        return run

    def sgu_mix(c):
        def run():
            parts = [_dot(jnp.where(causal, sw_ref[g], 0.0).astype(BF16), st["vn", c][:, g * GW:(g + 1) * GW])
                     for g in range(N_GROUPS)]
            gg = gg_ref[c * L:(c + 1) * L, :].astype(F32)
            st["yb", c] = (u_ref[c * L:(c + 1) * L, :].astype(F32) * (jnp.concatenate(parts, axis=1) + sbias_ref[...])
                           * (gg * jax.nn.sigmoid(gg))).astype(BF16)
        return run

    n_chunks = ya_ref.shape[0] // L
    _interleave([ya_piece(c) for c in range(n_col)],
                [f(c) for c in range(n_chunks) for f in (sgu_norm, sgu_mix)])
    ya = jnp.concatenate([st["ya", c] for c in range(n_col)], axis=1)
    yb_in = jnp.concatenate([st["yb", c] for c in range(n_chunks)], axis=0)
    _outproj_tail(ya, yb_in, ga_ref, gb_ref, h_ref, wb_ref, wo_ref, gn_ref, outs)


def _outproj(ya_in, yb_in, z, h, wpa, wpb, wout, gn, layer, last, sgu=None):
    n_tok = h.shape[0]
    tm = TM_OUT
    once = lambda shape, imap: pl.BlockSpec(shape, imap, pipeline_mode=pl.Buffered(1))
    wspec = lambda a: once(a.shape, lambda i: (0, 0))
    row = lambda w: pl.BlockSpec((tm, w), lambda i: (i, 0))
    zcol = lambda w, cb: pl.BlockSpec((tm, w), lambda i: (i, cb))
    tok = lambda dt: jax.ShapeDtypeStruct((n_tok, D_MODEL), dt)
    vspec = pl.BlockSpec((None, 1, D_G), lambda i: (layer, 0, 0))
    if sgu is None:
        kern, branch_specs, branch_ops = _outproj_kernel, [row(D_G)], (yb_in,)
    else:
        kern = _outproj_sgu_kernel
        branch_specs = [zcol(D_G, CB_U), zcol(D_G, CB_VG), zcol(D_G, CB_GG), vspec, vspec,
                        once((None, N_GROUPS, SGU_L, SGU_L), lambda i: (layer, 0, 0, 0)),
                        once((None, SGU_L, D_G), lambda i: (layer, 0, 0))]
        branch_ops = (z, z, z) + tuple(sgu)
    return pl.pallas_call(
        kern,
        grid=(n_tok // tm,),
        in_specs=[row(D_R)] + branch_specs + [
            zcol(D_MODEL, 0), zcol(D_MODEL, 1), row(D_MODEL),
            wspec(wpa), wspec(wpb), wspec(wout),
            pl.BlockSpec((None, 1, D_MODEL), lambda i: (gn[1], 0, 0)),
        ],
        out_specs=[row(D_MODEL)] if last else [row(D_MODEL), row(D_MODEL)],
        out_shape=[tok(F32)] if last else [tok(F32), tok(BF16)],
        compiler_params=pltpu.CompilerParams(dimension_semantics=("arbitrary",), vmem_limit_bytes=VMEM_LIMIT_PROJ),
        name="outproj",
    )(ya_in, *branch_ops, z, z, h, wpa, wpb, wout, gn[0])


def _block_tri(n, blk):
    i = jnp.arange(n)
    return ((i[:, None] >= i[None, :]) & (i[:, None] // blk == i[None, :] // blk)).astype(BF16)


def kernel(x_prompt, x_sample, state_wkv, state_shift, norm_g, w_in, shift_mu, w0, w2, a0, a2, k_k, k_a, r_k,
           lnx_g, lnx_b, sgu_ln_g, sgu_ln_b, sgu_w, sgu_b, w_proj_a, w_proj_b, w_out, final_norm_g):
    c_wd, c_gr = 3 * D_R, D_SHIFT
    proj_w = (w_proj_a, w_proj_b, w_out)

    zrow = jnp.zeros((DEPTH, D_R), F32)
    mu_wa = jnp.pad(shift_mu[:, c_wd:c_gr], ((0, 0), (0, D_R - 2 * R_W)))
    vec_rows = [shift_mu[:, 0:D_R], shift_mu[:, D_R:2 * D_R], shift_mu[:, 2 * D_R:3 * D_R], w0, a0, k_k, k_a,
                r_k.reshape(DEPTH, D_R), lnx_g, lnx_b, mu_wa] + [zrow] * (N_VEC - 11)
    vecs = jnp.stack(vec_rows, axis=1)
    zblk = jnp.zeros((DEPTH, R_W, D_R), F32)
    w2a2 = jnp.concatenate([jnp.concatenate([w2, zblk], axis=2),
                            jnp.concatenate([zblk, a2], axis=2)], axis=1).astype(BF16)
    norm_g3 = norm_g.reshape(DEPTH, 1, D_MODEL)
    final_g3 = final_norm_g.reshape(1, 1, D_MODEL)
    ln_g3, ln_b3 = sgu_ln_g.reshape(DEPTH, 1, D_G), sgu_ln_b.reshape(DEPTH, 1, D_G)
    bias_exp = jnp.repeat(jnp.swapaxes(sgu_b, 1, 2), GW, axis=2)
    tmask = jnp.tril(jnp.ones((DEC_SEQ, DEC_SEQ), F32))
    wc = jnp.repeat(jnp.transpose(sgu_w[:, :, :DEC_SEQ, :DEC_SEQ] * tmask, (0, 3, 2, 1)), GW, axis=3)
    bias_c = bias_exp[:, :DEC_SEQ, :]

    pmat = ((jnp.arange(GL)[:, None] // HEAD) == (jnp.arange(GL)[None, :] // HEAD)).astype(BF16)
    tri_p = _block_tri(WKV_L, WKV_L)
    tri_s = _block_tri(SAMPLE_NB * DEC_SEQ, DEC_SEQ)
    src_cols = jnp.asarray(Z_SRC_UNITS, jnp.int32)

    h_p = x_prompt.reshape(N_PROMPT, D_MODEL)
    h_s = x_sample.reshape(N_SAMPLE, D_MODEL)
    xn_p = xn_s = None
    stk_p = [jnp.zeros((DEPTH, BATCH, H_R, HEAD, HEAD), F32), jnp.zeros((DEPTH, BATCH, 1, D_SHIFT), F32)]
    wkv_s, shift_s = state_wkv, state_shift
    vn_s = jnp.zeros((DEPTH, N_SAMPLE, D_G), F32)
    for l in range(DEPTH):
        last = l == DEPTH - 1
        if l == 0:
            z_p, wpa, wpb, wout = _inproj(h_p, w_in, src_cols, l, TM_IN_NORM, norm_g3, cast=proj_w)
            z_s, = _inproj(h_s, w_in, src_cols, l, TM_IN_SAMPLE, norm_g3)
        else:
            z_p, wpa, wpb, wout = _inproj(xn_p, w_in, src_cols, l, TM_IN_PROMPT, cast=proj_w)
            z_s, = _inproj(xn_s, w_in, src_cols, l, TM_IN_SAMPLE)
        ya_p, *stk_p = _wkv_prompt(z_p.reshape(BATCH, SEQ, NZ), vecs, w2a2, tri_p, pmat, pmat, stk_p, l)
        ya_s, wkv_s, shift_s = _wkv_sample(z_s, shift_s, wkv_s, vecs, w2a2, tri_s, pmat, pmat, l)
        yb_s, vn_s = _sgu_sample(z_s, ln_g3, ln_b3, wc, bias_c, vn_s, l)
        gn = (final_g3, 0) if last else (norm_g3, l + 1)
        out_p = _outproj(ya_p.reshape(N_PROMPT, D_R), None, z_p, h_p, wpa, wpb, wout, gn, l, last,
                         sgu=(ln_g3, ln_b3, sgu_w, bias_exp))
        out_s = _outproj(ya_s, yb_s, z_s, h_s, wpa, wpb, wout, gn, l, last)
        if last:
            y_p, y_s = out_p[0], out_s[0]
        else:
            (h_p, xn_p), (h_s, xn_s) = out_p, out_s
    return (y_p.reshape(BATCH, SEQ, D_MODEL), y_s.reshape(DEC_BATCH, DEC_SEQ, D_MODEL),
            stk_p[0], stk_p[1], wkv_s, shift_s, vn_s.reshape(DEPTH, DEC_BATCH, DEC_SEQ, D_G))
```
